```python
import math
import jax
import jax.numpy as jnp
from jax import lax
import numpy as np

D_MODEL = 1024
BATCH = 16
SEQ = 4096
DEPTH = 4

HEAD_DIM = 64
N_HEADS = D_MODEL // HEAD_DIM
A_HEADS = (3 * N_HEADS) // 8
A_KV_HEADS = A_HEADS // 3
B_HEADS = N_HEADS // 4
C_HEADS = N_HEADS - A_HEADS - B_HEADS
DIFF_DIM = HEAD_DIM // 2
WINDOW = 128
A_BLOCK = 128
Q_BLOCK = 128
GRID_W = 64
NA_ROWS = 8
NA_COLS = 16
T5_BUCKETS = 32
T5_MAX_DIST = 128
D_FF = 4 * D_MODEL
ALPHA = (2 * DEPTH) ** 0.25
BETA = (8 * DEPTH) ** -0.25
EPS = 1e-5
NEG = -1e30
A_Q_W = A_HEADS * HEAD_DIM
A_KV_W = A_KV_HEADS * HEAD_DIM
B_W = B_HEADS * HEAD_DIM
C_W = C_HEADS * HEAD_DIM
IN_WIDTH = A_Q_W + 2 * A_KV_W + 3 * B_W + 3 * C_W

kernel_name = "hybrid_parallel_heads_encoder"


def layer_norm(x, g=None, b=None):
    xf = x.astype(jnp.float32)
    xc = xf - xf.mean(-1, keepdims=True)
    y = xc * lax.rsqrt((xc * xc).mean(-1, keepdims=True) + EPS)
    if g is not None:
        y = y * g.astype(jnp.float32) + b.astype(jnp.float32)
    return y.astype(x.dtype)


def t5_bucket(rel):
    half = T5_BUCKETS // 2
    exact = half // 2
    n = jnp.abs(rel)
    large = exact + (jnp.log(jnp.maximum(n, 1).astype(jnp.float32) / exact)
                     / math.log(T5_MAX_DIST / exact) * (half - exact)).astype(jnp.int32)
    large = jnp.minimum(large, half - 1)
    return (rel > 0).astype(jnp.int32) * half + jnp.where(n < exact, n, large)


def split_projection(proj):
    Bn, S, _ = proj.shape
    widths = (A_Q_W, A_KV_W, A_KV_W, B_W, B_W, B_W, C_W, C_W, C_W)
    heads = (A_HEADS, A_KV_HEADS, A_KV_HEADS, B_HEADS, B_HEADS, B_HEADS, C_HEADS, C_HEADS, C_HEADS)
    out = []
    off = 0
    for w, hn in zip(widths, heads):
        out.append(proj[..., off:off + w].reshape(Bn, S, hn, HEAD_DIM))
        off += w
    return out


def windowed_gqa(q, k, v, sink, bias_tab):
    Bn, S, HA, d = q.shape
    HKV = k.shape[2]
    G = HA // HKV
    nb = S // A_BLOCK
    qb = q.reshape(Bn, nb, A_BLOCK, HKV, G, d)

    def neighbours(t):
        tp = jnp.pad(t, ((0, 0), (A_BLOCK, A_BLOCK), (0, 0), (0, 0))).reshape(Bn, nb + 2, A_BLOCK, HKV, d)
        return jnp.concatenate([tp[:, :-2], tp[:, 1:-1], tp[:, 2:]], axis=2)

    kb, vb = neighbours(k), neighbours(v)
    s = jnp.einsum('bnqhgd,bnkhd->bhgnqk', qb, kb).astype(jnp.float32) * (d ** -0.5)
    qi = jnp.arange(A_BLOCK)
    kj = jnp.arange(3 * A_BLOCK) - A_BLOCK
    rel = kj[None, :] - qi[:, None]
    bias = bias_tab[t5_bucket(rel)].astype(jnp.float32)
    bias = bias.transpose(2, 0, 1).reshape(HKV, G, 1, A_BLOCK, 3 * A_BLOCK)
    kpos = jnp.arange(nb)[:, None] * A_BLOCK + kj[None, :]
    inside = (kpos >= 0) & (kpos < S)
    valid = (jnp.abs(rel) <= WINDOW)[None] & inside[:, None, :]
    s = jnp.where(valid, s + bias, NEG)
    sk = sink.astype(jnp.float32).reshape(HKV, G, 1, 1, 1)
    m = jnp.maximum(s.max(-1, keepdims=True), sk)
    p = jnp.exp(s - m)
    p = p / (p.sum(-1, keepdims=True) + jnp.exp(sk - m))
    o = jnp.einsum('bhgnqk,bnkhd->bnqhgd', p.astype(v.dtype), vb)
    return o.reshape(Bn, S, HA * d)


def diff_attention(q, k, v, lam, lam_init, subln_g, bias_tab):
    Bn, S, H, d = q.shape
    half = d // 2
    scale = half ** -0.5
    nq = S // Q_BLOCK
    qs = q.reshape(Bn, nq, Q_BLOCK, H, d).swapaxes(0, 1)
    k1, k2 = k[..., :half], k[..., half:]
    kpos = jnp.arange(S)

    def block(args):
        qblk, i = args
        qpos = i * Q_BLOCK + jnp.arange(Q_BLOCK)
        bias = bias_tab[t5_bucket(kpos[None, :] - qpos[:, None])].astype(jnp.float32).transpose(2, 0, 1)
        s1 = jnp.einsum('bqhd,bkhd->bhqk', qblk[..., :half], k1).astype(jnp.float32) * scale + bias
        s2 = jnp.einsum('bqhd,bkhd->bhqk', qblk[..., half:], k2).astype(jnp.float32) * scale + bias
        a = jax.nn.softmax(s1, axis=-1) - lam * jax.nn.softmax(s2, axis=-1)
        return jnp.einsum('bhqk,bkhd->bqhd', a.astype(v.dtype), v)

    o = lax.map(block, (qs, jnp.arange(nq)))
    o = o.swapaxes(0, 1).reshape(Bn, S, H, d).astype(jnp.float32)
    o = o * lax.rsqrt((o * o).mean(-1, keepdims=True) + EPS) * subln_g.astype(jnp.float32) * (1.0 - lam_init)
    return o.reshape(Bn, S, H * d).astype(v.dtype)


def neighbourhood_attention(q, k, v, rpb):
    Bn, S, H, d = q.shape
    rows = S // GRID_W
    kh = min(NA_ROWS, rows)
    kw = NA_COLS
    qg = q.reshape(Bn, rows, GRID_W, H, d)
    kg = k.reshape(Bn, rows, GRID_W, H, d)
    vg = v.reshape(Bn, rows, GRID_W, H, d)
    cols = jnp.arange(GRID_W)
    cstart = jnp.clip(cols - kw // 2, 0, GRID_W - kw)
    col_valid = (cols[None, :] >= cstart[:, None]) & (cols[None, :] < cstart[:, None] + kw)
    dc_idx = jnp.clip(cols[None, :] - cols[:, None] + kw - 1, 0, 2 * kw - 2)
    scale = d ** -0.5

    def row(r):
        rs = jnp.clip(r - kh // 2, 0, rows - kh)
        k_blk = lax.dynamic_slice_in_dim(kg, rs, kh, axis=1)
        v_blk = lax.dynamic_slice_in_dim(vg, rs, kh, axis=1)
        q_row = lax.dynamic_index_in_dim(qg, r, axis=1, keepdims=False)
        s = jnp.einsum('bqhd,brkhd->bhqrk', q_row, k_blk).astype(jnp.float32) * scale
        dr_idx = rs + jnp.arange(kh) - r + NA_ROWS - 1
        b = rpb[:, dr_idx][:, :, dc_idx].astype(jnp.float32).transpose(0, 2, 1, 3)
        s = jnp.where(col_valid[:, None, :], s + b, NEG)
        p = jax.nn.softmax(s, axis=(-2, -1))
        return jnp.einsum('bhqrk,brkhd->bqhd', p.astype(v.dtype), v_blk)

    o = lax.map(row, jnp.arange(rows))
    return o.swapaxes(0, 1).reshape(Bn, S, H * d)


def setup_inputs(seed: int = 0) -> dict:
    key = jax.random.key(seed)
    ks = jax.random.split(key, 16)
    nrm = jax.random.normal
    col_scale = np.ones((IN_WIDTH,), np.float32)
    col_scale[A_Q_W + A_KV_W:A_Q_W + 2 * A_KV_W] = BETA
    col_scale[A_Q_W + 2 * A_KV_W + 2 * B_W:A_Q_W + 2 * A_KV_W + 3 * B_W] = BETA
    col_scale[IN_WIDTH - C_W:] = BETA
    return {
        "x": nrm(ks[0], (BATCH, SEQ, D_MODEL), jnp.float32),
        "c": nrm(ks[1], (BATCH, D_MODEL), jnp.float32),
        "w_ada": nrm(ks[2], (DEPTH, D_MODEL, 6 * D_MODEL), jnp.float32) * (0.1 * D_MODEL ** -0.5),
        "b_ada": nrm(ks[3], (DEPTH, 6 * D_MODEL), jnp.float32) * 0.02,
        "w_in": nrm(ks[4], (DEPTH, D_MODEL, IN_WIDTH), jnp.float32) * (D_MODEL ** -0.5) * jnp.asarray(col_scale),
        "w_out": nrm(ks[5], (DEPTH, D_MODEL, D_MODEL), jnp.float32) * (D_MODEL ** -0.5 * BETA),
        "t5_bias": nrm(ks[6], (T5_BUCKETS, A_HEADS + B_HEADS), jnp.float32) * 0.5,
        "a_sink": nrm(ks[7], (DEPTH, A_HEADS), jnp.float32) * 0.5,
        "diff_lambda": nrm(ks[8], (DEPTH, 4, DIFF_DIM), jnp.float32) * 0.1,
        "diff_subln": 1.0 + 0.02 * nrm(ks[9], (DEPTH, HEAD_DIM), jnp.float32),
        "nat_rpb": nrm(ks[10], (DEPTH, C_HEADS, 2 * NA_ROWS - 1, 2 * NA_COLS - 1), jnp.float32) * 0.5,
        "ln_g": 1.0 + 0.02 * nrm(ks[11], (DEPTH, 2, D_MODEL), jnp.float32),
        "ln_b": 0.02 * nrm(ks[12], (DEPTH, 2, D_MODEL), jnp.float32),
        "w_ff1": nrm(ks[13], (DEPTH, D_MODEL, D_FF), jnp.float32) * (D_MODEL ** -0.5),
        "w_ff2": nrm(ks[14], (DEPTH, D_FF, D_MODEL), jnp.float32) * (D_FF ** -0.5 * BETA),
    }


def reference(x, c, w_ada, b_ada, w_in, w_out, t5_bias, a_sink, diff_lambda, diff_subln,
              nat_rpb, ln_g, ln_b, w_ff1, w_ff2):
    cs = jax.nn.silu(c)
    for l in range(DEPTH):
        mod = (cs @ w_ada[l] + b_ada[l])[:, None, :]
        sh1, sc1, g1, sh2, sc2, g2 = jnp.split(mod, 6, axis=-1)
        h = layer_norm(x) * (1 + sc1) + sh1
        qa, ka, va, qb, kb, vb, qc, kc, vc = split_projection(h @ w_in[l])
        lam_init = 0.8 - 0.6 * math.exp(-0.3 * l)
        lam_v = diff_lambda[l].astype(jnp.float32)
        lam = jnp.exp(jnp.sum(lam_v[0] * lam_v[1])) - jnp.exp(jnp.sum(lam_v[2] * lam_v[3])) + lam_init
        ya = windowed_gqa(qa, ka, va, a_sink[l], t5_bias[:, :A_HEADS])
        yb = diff_attention(qb, kb, vb, lam, lam_init, diff_subln[l], t5_bias[:, A_HEADS:])
        yc = neighbourhood_attention(qc, kc, vc, nat_rpb[l])
        y = jnp.concatenate([ya, yb, yc], axis=-1) @ w_out[l]
        x = layer_norm(ALPHA * x + (1 + g1) * y, ln_g[l, 0], ln_b[l, 0])
        h = layer_norm(x) * (1 + sc2) + sh2
        f = jnp.square(jax.nn.relu(h @ w_ff1[l])) @ w_ff2[l]
        x = layer_norm(ALPHA * x + (1 + g2) * f, ln_g[l, 1], ln_b[l, 1])
    return x
```

```python
import functools
import math

import jax
import jax.numpy as jnp
import numpy as np
from jax import lax
from jax.experimental import pallas as pl
from jax.experimental.pallas import tpu as pltpu

D_MODEL = 1024
DEPTH = 4
HEAD_DIM = 64
A_HEADS = 6
A_KV_HEADS = 2
B_HEADS = 4
C_HEADS = 6
DIFF_DIM = HEAD_DIM // 2
WINDOW = 128
GRID_W = 64
NA_ROWS = 8
NA_COLS = 16
T5_BUCKETS = 32
T5_MAX_DIST = 128
D_FF = 4 * D_MODEL
ALPHA = (2 * DEPTH) ** 0.25
EPS = 1e-5
NEG = -1e30
A_Q_W = A_HEADS * HEAD_DIM
A_KV_W = A_KV_HEADS * HEAD_DIM
B_W = B_HEADS * HEAD_DIM
C_W = C_HEADS * HEAD_DIM
MAIN_W = A_Q_W + A_KV_W + 2 * B_W + 2 * C_W
VT_W = A_KV_W + B_W + C_W

LANE = 128
SUBLANE = 8
TQ = 256
KCH = 256
TM = 512
FF_CHUNK = 1024
A_KEYS = TQ + 2 * WINDOW
C_QROWS = TQ // GRID_W
C_WROWS = C_QROWS + NA_ROWS
C_KEYS = C_WROWS * GRID_W

SCALE_A = HEAD_DIM ** -0.5
SCALE_B = DIFF_DIM ** -0.5
SCALE_C = HEAD_DIM ** -0.5
LOG2E = math.log2(math.e)

A_PERM = (0, 3, 1, 4, 2, 5)

_MXU_DTYPE = jnp.bfloat16
_F32 = jnp.float32
_VMEM_LIMIT = 56 * 1024 * 1024


def _params(sem, vmem=_VMEM_LIMIT):
    return pltpu.CompilerParams(dimension_semantics=sem, vmem_limit_bytes=vmem)


def _ln_rows(x):
    mu = jnp.mean(x, axis=-1, keepdims=True)
    xc = x - mu
    var = jnp.mean(xc * xc, axis=-1, keepdims=True)
    return xc * lax.rsqrt(var + EPS)


def _ada_kernel(c_ref, w_ref, b_ref, o_ref):
    c = c_ref[...]
    cs = c * (1.0 / (1.0 + jnp.exp(-c)))
    o_ref[0] = jnp.dot(cs.astype(_MXU_DTYPE), w_ref[0], preferred_element_type=_F32) + b_ref[0]


def _ada(c, w_ada, b_ada):
    depth, d, n = w_ada.shape
    bsz = c.shape[0]
    tn = 1024
    return pl.pallas_call(
        _ada_kernel,
        grid=(depth, n // tn),
        in_specs=[
            pl.BlockSpec((bsz, d), lambda l, j: (0, 0)),
            pl.BlockSpec((1, d, tn), lambda l, j: (l, 0, j)),
            pl.BlockSpec((1, 1, tn), lambda l, j: (l, 0, j)),
        ],
        out_specs=pl.BlockSpec((1, bsz, tn), lambda l, j: (l, 0, j)),
        out_shape=jax.ShapeDtypeStruct((depth, bsz, n), _F32),
        compiler_params=_params(("parallel", "parallel")),
        name="ada_mod",
    )(c, w_ada, b_ada.reshape(depth, 1, n))


def _inproj_kernel(x_ref, sc_ref, sh_ref, wm_ref, wvt_ref,
                   qa_ref, ka_ref, qb_ref, kb_ref, qc_ref, kc_ref,
                   vat_ref, vbt_ref, vct_ref):
    h = (_ln_rows(x_ref[...]) * (1.0 + sc_ref[0]) + sh_ref[0]).astype(_MXU_DTYPE)
    main = jnp.dot(h, wm_ref[...], preferred_element_type=_F32).astype(qa_ref.dtype)
    off = 0
    for ref in (qa_ref, ka_ref, qb_ref, kb_ref, qc_ref, kc_ref):
        w = ref.shape[-1]
        ref[...] = main[:, off:off + w]
        off += w
    vt = lax.dot_general(wvt_ref[...], h, (((1,), (1,)), ((), ())),
                         preferred_element_type=_F32).astype(vat_ref.dtype)
    for j in range(vt.shape[1] // LANE):
        blk = vt[:, j * LANE:(j + 1) * LANE]
        vat_ref[j] = blk[0:A_KV_W]
        vbt_ref[j] = blk[A_KV_W:A_KV_W + B_W]
        vct_ref[j] = blk[A_KV_W + B_W:VT_W]


def _inproj(x, sc, sh, wm, wvt, seq):
    t, d = x.shape
    dt = _MXU_DTYPE
    tok = lambda w: pl.BlockSpec((TM, w), lambda i: (i, 0))
    vtb = lambda w: pl.BlockSpec((TM // LANE, w, LANE), lambda i: (i, 0, 0))
    mod = pl.BlockSpec((1, 1, d), lambda i: ((i * TM) // seq, 0, 0))
    widths = (A_Q_W, A_KV_W, B_W, B_W, C_W, C_W)
    vwidths = (A_KV_W, B_W, C_W)
    return pl.pallas_call(
        _inproj_kernel,
        grid=(t // TM,),
        in_specs=[tok(d), mod, mod,
                  pl.BlockSpec((d, MAIN_W), lambda i: (0, 0)),
                  pl.BlockSpec((VT_W, d), lambda i: (0, 0))],
        out_specs=[tok(w) for w in widths] + [vtb(w) for w in vwidths],
        out_shape=[jax.ShapeDtypeStruct((t, w), dt) for w in widths]
        + [jax.ShapeDtypeStruct((t // LANE, w, LANE), dt) for w in vwidths],
        compiler_params=_params(("parallel",)),
        name="ln_inproj",
    )(x, sc, sh, wm, wvt)


def _transpose_bf16(x):
    return x.astype(_F32).T.astype(x.dtype)


def _row_band(x, lo, width):
    row = lax.broadcasted_iota(jnp.int32, x.shape, 0)
    return jnp.where((row >= lo) & (row < lo + width), x, jnp.zeros_like(x))


def _fold_rows(x, op):
    return op(x.reshape(x.shape[0] // SUBLANE, SUBLANE, x.shape[1]), axis=0)


def _attn_a_kernel(sink_ref, q_ref, k_ref, vt_ref, bias_ref, o_ref, *, seq):
    i = pl.program_id(1)
    ws = pl.multiple_of(jnp.clip(i * TQ - WINDOW, 0, seq - A_KEYS), LANE)
    kwin = k_ref[pl.ds(ws, A_KEYS), :]
    wb = ws // LANE
    vwin_t = jnp.concatenate([vt_ref[wb + j] for j in range(A_KEYS // LANE)], axis=1)
    q_t = _transpose_bf16(q_ref[...])
    c2 = SCALE_A * LOG2E
    outs = []
    for g in range(A_HEADS // A_KV_HEADS):
        q_tg = q_t[g * LANE:(g + 1) * LANE]
        for h in range(A_KV_HEADS):
            n = A_KV_HEADS * g + h
            qm = _row_band(q_tg, h * HEAD_DIM, HEAD_DIM)
            t = jnp.dot(kwin, qm, preferred_element_type=_F32) + bias_ref[0, n]
            sk = sink_ref[n] * (1.0 / SCALE_A)
            m = jnp.maximum(jnp.max(t, axis=0, keepdims=True), sk)
            p = jnp.exp2((t - m) * c2)
            den = jnp.sum(p, axis=0, keepdims=True) + jnp.exp2((sk - m) * c2)
            o_t = jnp.dot(vwin_t[h * HEAD_DIM:(h + 1) * HEAD_DIM], p.astype(_MXU_DTYPE),
                          preferred_element_type=_F32)
            outs.append(o_t / den)
    o_ref[...] = jnp.concatenate(outs, axis=0).T.astype(o_ref.dtype)


def _attn_a(qa, ka, vat, bias, sink, bsz, seq):
    nq = seq // TQ
    case = lambda i: (i > 0).astype(jnp.int32) + (i == nq - 1).astype(jnp.int32)
    return pl.pallas_call(
        functools.partial(_attn_a_kernel, seq=seq),
        grid=(bsz, nq),
        in_specs=[
            pl.BlockSpec(memory_space=pltpu.SMEM),
            pl.BlockSpec((TQ, A_Q_W), lambda b, i: (b * nq + i, 0)),
            pl.BlockSpec((seq, A_KV_W), lambda b, i: (b, 0)),
            pl.BlockSpec((seq // LANE, A_KV_W, LANE), lambda b, i: (b, 0, 0)),
            pl.BlockSpec((1, A_HEADS, A_KEYS, TQ), lambda b, i: (case(i), 0, 0, 0)),
        ],
        out_specs=pl.BlockSpec((TQ, A_Q_W), lambda b, i: (b * nq + i, 0)),
        out_shape=jax.ShapeDtypeStruct(qa.shape, qa.dtype),
        compiler_params=_params(("parallel", "arbitrary")),
        name="attn_window",
    )(sink, qa, ka, vat, bias)


def _attn_b_kernel(q_ref, k_ref, vt_ref, bias_ref, lam_ref, g_ref, o_ref, s_scr, p_scr,
                   *, seq, lam_init):
    i = pl.program_id(1)
    nck = seq // KCH
    c2 = SCALE_B * LOG2E
    q_t = _transpose_bf16(q_ref[...])
    lv = lam_ref[...]
    lam = (jnp.exp(jnp.sum(lv[0:1] * lv[1:2], axis=-1, keepdims=True))
           - jnp.exp(jnp.sum(lv[2:3] * lv[3:4], axis=-1, keepdims=True)) + lam_init)
    outs = []
    for h in range(B_HEADS):
        v_t = jnp.concatenate([vt_ref[blk, h * HEAD_DIM:(h + 1) * HEAD_DIM, :]
                               for blk in range(seq // LANE)], axis=1)
        maps = []
        for j in range(2):
            n = 2 * h + j
            buf = n % 2
            qm = _row_band(q_t, n * DIFF_DIM, DIFF_DIM)
            macc = jnp.full((SUBLANE, TQ), -jnp.inf, _F32)
            for cc in range(nck // 2):
                s = jnp.dot(k_ref[cc * 2 * KCH:(cc + 1) * 2 * KCH, :], qm,
                            preferred_element_type=_F32)
                for hf in range(2):
                    c = 2 * cc + hf
                    idx = jnp.clip(c - i, -2, 2) + 2
                    t = s[hf * KCH:(hf + 1) * KCH] + bias_ref[h, idx]
                    s_scr[buf, c * KCH:(c + 1) * KCH, :] = t
                    macc = jnp.maximum(macc, _fold_rows(t, jnp.max))
            m = jnp.max(macc, axis=0, keepdims=True)
            lacc = jnp.zeros((SUBLANE, TQ), _F32)
            for c in range(nck):
                t = s_scr[buf, c * KCH:(c + 1) * KCH, :]
                p = jnp.exp2((t - m) * c2)
                lacc = lacc + _fold_rows(p, jnp.sum)
                p_scr[buf, c * KCH:(c + 1) * KCH, :] = p.astype(p_scr.dtype)
            den = jnp.sum(lacc, axis=0, keepdims=True)
            o_t = jnp.dot(v_t, p_scr[buf], preferred_element_type=_F32)
            maps.append(o_t / den)
        o = maps[0] - lam * maps[1]
        o = o * lax.rsqrt(jnp.mean(o * o, axis=0, keepdims=True) + EPS)
        outs.append(o * g_ref[...] * (1.0 - lam_init))
    o_ref[...] = jnp.concatenate(outs, axis=0).T.astype(o_ref.dtype)


def _attn_b(qb, kb, vbt, bias, lam_v, gain, bsz, seq, lam_init):
    nq = seq // TQ
    return pl.pallas_call(
        functools.partial(_attn_b_kernel, seq=seq, lam_init=lam_init),
        grid=(bsz, nq),
        in_specs=[
            pl.BlockSpec((TQ, B_W), lambda b, i: (b * nq + i, 0)),
            pl.BlockSpec((seq, B_W), lambda b, i: (b, 0)),
            pl.BlockSpec((seq // LANE, B_W, LANE), lambda b, i: (b, 0, 0)),
            pl.BlockSpec(bias.shape, lambda b, i: (0, 0, 0, 0)),
            pl.BlockSpec(lam_v.shape, lambda b, i: (0, 0)),
            pl.BlockSpec(gain.shape, lambda b, i: (0, 0)),
        ],
        out_specs=pl.BlockSpec((TQ, B_W), lambda b, i: (b * nq + i, 0)),
        out_shape=jax.ShapeDtypeStruct(qb.shape, qb.dtype),
        scratch_shapes=[pltpu.VMEM((2, seq, TQ), _F32), pltpu.VMEM((2, seq, TQ), _MXU_DTYPE)],
        compiler_params=_params(("parallel", "arbitrary")),
        name="attn_diff",
    )(qb, kb, vbt, bias, lam_v, gain)


def _attn_c_kernel(q_ref, k_ref, vt_ref, bias_ref, o_ref, *, seq):
    i = pl.program_id(1)
    rows = seq // GRID_W
    wrow = jnp.clip(i * C_QROWS - NA_ROWS // 2, 0, rows - C_WROWS)
    ws = pl.multiple_of(wrow * GRID_W, TQ)
    kwin = k_ref[pl.ds(ws, C_KEYS), :]
    wb = ws // LANE
    vwin_t = jnp.concatenate([vt_ref[wb + j] for j in range(C_KEYS // LANE)], axis=1)
    q_t = _transpose_bf16(q_ref[...])
    c2 = SCALE_C * LOG2E
    outs = []
    for n in range(C_HEADS):
        g, h = divmod(n, 2)
        qm = _row_band(q_t[g * LANE:(g + 1) * LANE], h * HEAD_DIM, HEAD_DIM)
        t = jnp.dot(kwin[:, g * LANE:(g + 1) * LANE], qm,
                    preferred_element_type=_F32) + bias_ref[0, n]
        m = jnp.max(t, axis=0, keepdims=True)
        p = jnp.exp2((t - m) * c2)
        den = jnp.sum(p, axis=0, keepdims=True)
        o_t = jnp.dot(vwin_t[n * HEAD_DIM:(n + 1) * HEAD_DIM], p.astype(_MXU_DTYPE),
                      preferred_element_type=_F32)
        outs.append(o_t / den)
    o_ref[...] = jnp.concatenate(outs, axis=0).T.astype(o_ref.dtype)


def _attn_c(qc, kc, vct, bias, bsz, seq):
    nq = seq // TQ
    case = lambda i: (i > 0).astype(jnp.int32) + (i == nq - 1).astype(jnp.int32)
    return pl.pallas_call(
        functools.partial(_attn_c_kernel, seq=seq),
        grid=(bsz, nq),
        in_specs=[
            pl.BlockSpec((TQ, C_W), lambda b, i: (b * nq + i, 0)),
            pl.BlockSpec((seq, C_W), lambda b, i: (b, 0)),
            pl.BlockSpec((seq // LANE, C_W, LANE), lambda b, i: (b, 0, 0)),
            pl.BlockSpec((1, C_HEADS, C_KEYS, TQ), lambda b, i: (case(i), 0, 0, 0)),
        ],
        out_specs=pl.BlockSpec((TQ, C_W), lambda b, i: (b * nq + i, 0)),
        out_shape=jax.ShapeDtypeStruct(qc.shape, qc.dtype),
        compiler_params=_params(("parallel", "arbitrary")),
        name="attn_nbr",
    )(qc, kc, vct, bias)


def _outproj_kernel(oa_ref, ob_ref, oc_ref, x_ref, g_ref, wo_ref, lng_ref, lnb_ref, out_ref):
    o = jnp.concatenate([oa_ref[...], ob_ref[...], oc_ref[...]], axis=-1)
    y = jnp.dot(o, wo_ref[...], preferred_element_type=_F32)
    z = ALPHA * x_ref[...] + (1.0 + g_ref[0]) * y
    out_ref[...] = _ln_rows(z) * lng_ref[...] + lnb_ref[...]


def _outproj(oa, ob, oc, x, gate, wo, lng, lnb, seq):
    t, d = x.shape
    tok = lambda w: pl.BlockSpec((TM, w), lambda i: (i, 0))
    mod = pl.BlockSpec((1, 1, d), lambda i: ((i * TM) // seq, 0, 0))
    vec = pl.BlockSpec((1, d), lambda i: (0, 0))
    return pl.pallas_call(
        _outproj_kernel,
        grid=(t // TM,),
        in_specs=[tok(A_Q_W), tok(B_W), tok(C_W), tok(d), mod,
                  pl.BlockSpec((d, d), lambda i: (0, 0)), vec, vec],
        out_specs=tok(d),
        out_shape=jax.ShapeDtypeStruct((t, d), _F32),
        compiler_params=_params(("parallel",)),
        name="outproj_ln",
    )(oa, ob, oc, x, gate, wo, lng, lnb)


def _ffn_kernel(x_ref, sc_ref, sh_ref, g_ref, w1_ref, w2_ref, lng_ref, lnb_ref, out_ref):
    x = x_ref[...]
    h = (_ln_rows(x) * (1.0 + sc_ref[0]) + sh_ref[0]).astype(_MXU_DTYPE)
    f = jnp.zeros(x.shape, _F32)
    for c in range(w1_ref.shape[1] // FF_CHUNK):
        u = jnp.dot(h, w1_ref[:, c * FF_CHUNK:(c + 1) * FF_CHUNK], preferred_element_type=_F32)
        u = jnp.square(jnp.maximum(u, 0.0)).astype(_MXU_DTYPE)
        f = f + jnp.dot(u, w2_ref[c * FF_CHUNK:(c + 1) * FF_CHUNK, :], preferred_element_type=_F32)
    z = ALPHA * x + (1.0 + g_ref[0]) * f
    out_ref[...] = _ln_rows(z) * lng_ref[...] + lnb_ref[...]


def _ffn(x, sc, sh, gate, w1, w2, lng, lnb, seq):
    t, d = x.shape
    dff = w1.shape[1]
    tok = pl.BlockSpec((TM, d), lambda i: (i, 0))
    mod = pl.BlockSpec((1, 1, d), lambda i: ((i * TM) // seq, 0, 0))
    vec = pl.BlockSpec((1, d), lambda i: (0, 0))
    return pl.pallas_call(
        _ffn_kernel,
        grid=(t // TM,),
        in_specs=[tok, mod, mod, mod,
                  pl.BlockSpec((d, dff), lambda i: (0, 0)),
                  pl.BlockSpec((dff, d), lambda i: (0, 0)), vec, vec],
        out_specs=tok,
        out_shape=jax.ShapeDtypeStruct((t, d), _F32),
        compiler_params=_params(("parallel",)),
        name="ffn_ln",
    )(x, sc, sh, gate, w1, w2, lng, lnb)


def _t5_bucket(rel):
    half = T5_BUCKETS // 2
    exact = half // 2
    n = jnp.abs(rel)
    large = exact + (jnp.log(jnp.maximum(n, 1).astype(jnp.float32) / exact)
                     / math.log(T5_MAX_DIST / exact) * (half - exact)).astype(jnp.int32)
    large = jnp.minimum(large, half - 1)
    return (rel > 0).astype(jnp.int32) * half + jnp.where(n < exact, n, large)


def _bias_tables_a(tab):
    j = jnp.arange(A_KEYS)[:, None]
    q = jnp.arange(TQ)[None, :]
    offs = jnp.asarray([0, -WINDOW, -2 * WINDOW])
    rel = offs[:, None, None] + j - q
    vals = tab.astype(_F32)[_t5_bucket(rel)]
    vals = jnp.where((jnp.abs(rel) <= WINDOW)[..., None], vals, NEG)
    vals = vals[..., jnp.asarray(A_PERM)]
    return vals.transpose(0, 3, 1, 2) * (1.0 / SCALE_A)


def _bias_tables_b(tab):
    kk = jnp.arange(KCH)[:, None]
    q = jnp.arange(TQ)[None, :]
    d = jnp.arange(-2, 3)
    rel = d[:, None, None] * KCH + kk - q
    vals = tab.astype(_F32)[_t5_bucket(rel)]
    return vals.transpose(3, 0, 1, 2) * (1.0 / SCALE_B)


def _bias_tables_c(rpb, rows):
    kh = min(NA_ROWS, rows)
    kc = jnp.arange(GRID_W)[:, None]
    qc = jnp.arange(GRID_W)[None, :]
    cstart = jnp.clip(qc - NA_COLS // 2, 0, GRID_W - NA_COLS)
    col_valid = (kc >= cstart) & (kc < cstart + NA_COLS)
    dc = jnp.clip(kc - qc + NA_COLS - 1, 0, 2 * NA_COLS - 2)
    out = []
    for r0, w0 in ((0, 0), (NA_ROWS // 2, 0), (rows - C_QROWS, rows - C_WROWS)):
        r = r0 + jnp.arange(C_QROWS)[None, :]
        kr = w0 + jnp.arange(C_WROWS)[:, None]
        rs = jnp.clip(r - kh // 2, 0, rows - kh)
        row_valid = (kr >= rs) & (kr < rs + kh)
        dr = jnp.clip(kr - r + NA_ROWS - 1, 0, 2 * NA_ROWS - 2)
        vals = rpb.astype(_F32)[:, dr[:, None, :, None], dc[None, :, None, :]]
        valid = row_valid[:, None, :, None] & col_valid[None, :, None, :]
        out.append(jnp.where(valid[None], vals, NEG).reshape(C_HEADS, C_KEYS, TQ))
    return jnp.stack(out) * (1.0 / SCALE_C)


def _split_w_in(w):
    offs = np.cumsum([0, A_Q_W, A_KV_W, A_KV_W, B_W, B_W, B_W, C_W, C_W, C_W])
    qa, ka, va, qb, kb, vb, qc, kc, vc = [w[:, offs[i]:offs[i + 1]] for i in range(9)]
    qa = qa.reshape(-1, A_HEADS, HEAD_DIM)[:, jnp.asarray(A_PERM)].reshape(-1, A_Q_W)
    main = jnp.concatenate([qa, ka, qb, kb, qc, kc], axis=1)
    vt = jnp.concatenate([va, vb, vc], axis=1).T
    return main.astype(_MXU_DTYPE), vt.astype(_MXU_DTYPE)


def _permute_w_out(w):
    wa = w[:A_Q_W].reshape(A_HEADS, HEAD_DIM, -1)[jnp.asarray(A_PERM)].reshape(A_Q_W, -1)
    return jnp.concatenate([wa, w[A_Q_W:]], axis=0).astype(_MXU_DTYPE)


def kernel(x, c, w_ada, b_ada, w_in, w_out, t5_bias, a_sink, diff_lambda, diff_subln,
           nat_rpb, ln_g, ln_b, w_ff1, w_ff2):
    bsz, seq, d = x.shape
    depth = w_in.shape[0]
    assert d == D_MODEL and seq % TM == 0 and seq >= C_KEYS and seq >= A_KEYS
    rows = seq // GRID_W

    mod = _ada(c, w_ada.astype(_MXU_DTYPE), b_ada)
    bias_a = _bias_tables_a(t5_bias[:, :A_HEADS])
    bias_b = _bias_tables_b(t5_bias[:, A_HEADS:])

    xt = x.reshape(bsz * seq, d)
    for l in range(depth):
        sh1, sc1, g1, sh2, sc2, g2 = [mod[l, :, k * d:(k + 1) * d].reshape(bsz, 1, d) for k in range(6)]
        wm, wvt = _split_w_in(w_in[l])
        qa, ka, qb, kb, qc, kc, vat, vbt, vct = _inproj(xt, sc1, sh1, wm, wvt, seq)

        lam_init = 0.8 - 0.6 * math.exp(-0.3 * l)
        gain = jnp.broadcast_to(diff_subln[l].astype(_F32)[:, None], (HEAD_DIM, TQ))
        oa = _attn_a(qa, ka, vat, bias_a, a_sink[l][jnp.asarray(A_PERM)].astype(_F32), bsz, seq)
        ob = _attn_b(qb, kb, vbt, bias_b, diff_lambda[l].astype(_F32), gain, bsz, seq, lam_init)
        oc = _attn_c(qc, kc, vct, _bias_tables_c(nat_rpb[l], rows), bsz, seq)

        x1 = _outproj(oa, ob, oc, xt, g1, _permute_w_out(w_out[l]),
                      ln_g[l, 0].reshape(1, d), ln_b[l, 0].reshape(1, d), seq)
        xt = _ffn(x1, sc2, sh2, g2, w_ff1[l].astype(_MXU_DTYPE), w_ff2[l].astype(_MXU_DTYPE),
                  ln_g[l, 1].reshape(1, d), ln_b[l, 1].reshape(1, d), seq)
    return xt.reshape(bsz, seq, d)
```

```python
import functools
import math

import jax
import jax.numpy as jnp
import numpy as np
from jax import lax
from jax.experimental import pallas as pl
from jax.experimental.pallas import tpu as pltpu

D_MODEL = 1024
DEPTH = 4
HEAD_DIM = 64
A_HEADS = 6
A_KV_HEADS = 2
B_HEADS = 4
C_HEADS = 6
DIFF_DIM = HEAD_DIM // 2
WINDOW = 128
GRID_W = 64
NA_ROWS = 8
NA_COLS = 16
T5_BUCKETS = 32
T5_MAX_DIST = 128
D_FF = 4 * D_MODEL
ALPHA = (2 * DEPTH) ** 0.25
EPS = 1e-5
NEG = -1e30
A_Q_W = A_HEADS * HEAD_DIM
A_KV_W = A_KV_HEADS * HEAD_DIM
B_W = B_HEADS * HEAD_DIM
C_W = C_HEADS * HEAD_DIM
MAIN_W = A_Q_W + A_KV_W + 2 * B_W + 2 * C_W
VT_W = A_KV_W + B_W + C_W

LANE = 128
SUBLANE = 8
TQ = 256
KCH = 256
TM = 512
FF_CHUNK = 1024
A_KEYS = TQ + 2 * WINDOW
C_QROWS = TQ // GRID_W
C_WROWS = C_QROWS + NA_ROWS
C_KEYS = C_WROWS * GRID_W

ONES_ROWS = 16

SCALE_A = HEAD_DIM ** -0.5
SCALE_B = DIFF_DIM ** -0.5
SCALE_C = HEAD_DIM ** -0.5
LOG2E = math.log2(math.e)

A_PERM = (0, 3, 1, 4, 2, 5)

_MXU_DTYPE = jnp.bfloat16
_F32 = jnp.float32
_VMEM_LIMIT = 56 * 1024 * 1024


def _params(sem, vmem=_VMEM_LIMIT):
    return pltpu.CompilerParams(dimension_semantics=sem, vmem_limit_bytes=vmem)


def _ln_rows(x):
    mu = jnp.mean(x, axis=-1, keepdims=True)
    xc = x - mu
    var = jnp.mean(xc * xc, axis=-1, keepdims=True)
    return xc * lax.rsqrt(var + EPS)


def _ada_kernel(c_ref, w_ref, b_ref, o_ref):
    c = c_ref[...]
    cs = c * (1.0 / (1.0 + jnp.exp(-c)))
    o_ref[0] = jnp.dot(cs.astype(_MXU_DTYPE), w_ref[0], preferred_element_type=_F32) + b_ref[0]


def _ada(c, w_ada, b_ada):
    depth, d, n = w_ada.shape
    bsz = c.shape[0]
    tn = 1024
    return pl.pallas_call(
        _ada_kernel,
        grid=(depth, n // tn),
        in_specs=[
            pl.BlockSpec((bsz, d), lambda l, j: (0, 0)),
            pl.BlockSpec((1, d, tn), lambda l, j: (l, 0, j)),
            pl.BlockSpec((1, 1, tn), lambda l, j: (l, 0, j)),
        ],
        out_specs=pl.BlockSpec((1, bsz, tn), lambda l, j: (l, 0, j)),
        out_shape=jax.ShapeDtypeStruct((depth, bsz, n), _F32),
        compiler_params=_params(("parallel", "parallel")),
        name="ada_mod",
    )(c, w_ada, b_ada.reshape(depth, 1, n))


def _inproj_kernel(x_ref, sc_ref, sh_ref, wm_ref, wvt_ref,
                   qa_ref, ka_ref, qb_ref, kb_ref, qc_ref, kc_ref,
                   vat_ref, vbt_ref, vct_ref):
    h = (_ln_rows(x_ref[...]) * (1.0 + sc_ref[0]) + sh_ref[0]).astype(_MXU_DTYPE)
    main = jnp.dot(h, wm_ref[...], preferred_element_type=_F32).astype(qa_ref.dtype)
    off = 0
    for ref in (qa_ref, ka_ref, qb_ref, kb_ref, qc_ref, kc_ref):
        w = ref.shape[-1]
        ref[...] = main[:, off:off + w]
        off += w
    vt = lax.dot_general(wvt_ref[...], h, (((1,), (1,)), ((), ())),
                         preferred_element_type=_F32).astype(vat_ref.dtype)
    for j in range(vt.shape[1] // LANE):
        blk = vt[:, j * LANE:(j + 1) * LANE]
        vat_ref[j] = blk[0:A_KV_W]
        vbt_ref[j] = blk[A_KV_W:A_KV_W + B_W]
        vct_ref[j] = blk[A_KV_W + B_W:VT_W]


def _inproj(x, sc, sh, wm, wvt, seq):
    t, d = x.shape
    dt = _MXU_DTYPE
    tok = lambda w: pl.BlockSpec((TM, w), lambda i: (i, 0))
    vtb = lambda w: pl.BlockSpec((TM // LANE, w, LANE), lambda i: (i, 0, 0))
    mod = pl.BlockSpec((1, 1, d), lambda i: ((i * TM) // seq, 0, 0))
    widths = (A_Q_W, A_KV_W, B_W, B_W, C_W, C_W)
    vwidths = (A_KV_W, B_W, C_W)
    return pl.pallas_call(
        _inproj_kernel,
        grid=(t // TM,),
        in_specs=[tok(d), mod, mod,
                  pl.BlockSpec((d, MAIN_W), lambda i: (0, 0)),
                  pl.BlockSpec((VT_W, d), lambda i: (0, 0))],
        out_specs=[tok(w) for w in widths] + [vtb(w) for w in vwidths],
        out_shape=[jax.ShapeDtypeStruct((t, w), dt) for w in widths]
        + [jax.ShapeDtypeStruct((t // LANE, w, LANE), dt) for w in vwidths],
        compiler_params=_params(("parallel",)),
        name="ln_inproj",
    )(x, sc, sh, wm, wvt)


def _transpose_bf16(x):
    return x.astype(_F32).T.astype(x.dtype)


def _row_band(x, lo, width):
    row = lax.broadcasted_iota(jnp.int32, x.shape, 0)
    return jnp.where((row >= lo) & (row < lo + width), x, jnp.zeros_like(x))


def _with_ones_rows(v_t):
    ones = jnp.ones((ONES_ROWS, v_t.shape[1]), v_t.dtype)
    return jnp.concatenate([v_t, ones], axis=0)


def _fold_rows(x, op):
    return op(x.reshape(x.shape[0] // SUBLANE, SUBLANE, x.shape[1]), axis=0)


def _attn_a_kernel(sink_ref, q_ref, k_ref, vt_ref, bias_ref, o_ref, *, seq):
    i = pl.program_id(1)
    ws = pl.multiple_of(jnp.clip(i * TQ - WINDOW, 0, seq - A_KEYS), LANE)
    kwin = k_ref[pl.ds(ws, A_KEYS), :]
    wb = ws // LANE
    vwin_t = jnp.concatenate([vt_ref[wb + j] for j in range(A_KEYS // LANE)], axis=1)
    q_t = _transpose_bf16(q_ref[...])
    ts = []
    for n in range(A_HEADS):
        g, h = divmod(n, A_KV_HEADS)
        qm = _row_band(q_t[g * LANE:(g + 1) * LANE], h * HEAD_DIM, HEAD_DIM)
        ts.append(jnp.dot(kwin, qm, preferred_element_type=_F32) + bias_ref[0, n])
    outs = []
    for n in range(A_HEADS):
        h = n % A_KV_HEADS
        sk = sink_ref[n]
        m = jnp.maximum(jnp.max(ts[n], axis=0, keepdims=True), sk)
        p = jnp.exp2(ts[n] - m).astype(_MXU_DTYPE)
        o_t = jnp.dot(_with_ones_rows(vwin_t[h * HEAD_DIM:(h + 1) * HEAD_DIM]), p,
                      preferred_element_type=_F32)
        den = o_t[HEAD_DIM:HEAD_DIM + 1] + jnp.exp2(sk - m)
        outs.append(o_t[:HEAD_DIM] / den)
    o_ref[...] = jnp.concatenate(outs, axis=0).T.astype(o_ref.dtype)


def _attn_a(qa, ka, vat, bias, sink, bsz, seq):
    nq = seq // TQ
    case = lambda i: (i > 0).astype(jnp.int32) + (i == nq - 1).astype(jnp.int32)
    return pl.pallas_call(
        functools.partial(_attn_a_kernel, seq=seq),
        grid=(bsz, nq),
        in_specs=[
            pl.BlockSpec(memory_space=pltpu.SMEM),
            pl.BlockSpec((TQ, A_Q_W), lambda b, i: (b * nq + i, 0)),
            pl.BlockSpec((seq, A_KV_W), lambda b, i: (b, 0)),
            pl.BlockSpec((seq // LANE, A_KV_W, LANE), lambda b, i: (b, 0, 0)),
            pl.BlockSpec((1, A_HEADS, A_KEYS, TQ), lambda b, i: (case(i), 0, 0, 0)),
        ],
        out_specs=pl.BlockSpec((TQ, A_Q_W), lambda b, i: (b * nq + i, 0)),
        out_shape=jax.ShapeDtypeStruct(qa.shape, qa.dtype),
        compiler_params=_params(("parallel", "arbitrary")),
        name="attn_window",
    )(sink, qa, ka, vat, bias)


def _attn_b_kernel(q_ref, k_ref, vt_ref, bias_ref, lam_ref, g_ref, o_ref, s_scr, p_scr,
                   *, seq, lam_init):
    i = pl.program_id(1)
    nck = seq // KCH
    nmaps = 2 * B_HEADS
    q_t = _transpose_bf16(q_ref[...])
    qms = [_row_band(q_t, n * DIFF_DIM, DIFF_DIM) for n in range(nmaps)]
    lv = lam_ref[...]
    lam = (jnp.exp(jnp.sum(lv[0:1] * lv[1:2], axis=-1, keepdims=True))
           - jnp.exp(jnp.sum(lv[2:3] * lv[3:4], axis=-1, keepdims=True)) + lam_init)

    def score_chunk(n, c):
        idx = jnp.clip(c - i, -2, 2) + 2
        t = jnp.dot(k_ref[c * KCH:(c + 1) * KCH, :], qms[n],
                    preferred_element_type=_F32) + bias_ref[n // 2, idx]
        s_scr[n % 2, c * KCH:(c + 1) * KCH, :] = t
        return _fold_rows(t, jnp.max)

    def prob_chunk(n, c, m):
        t = s_scr[n % 2, c * KCH:(c + 1) * KCH, :]
        p_scr[n % 2, c * KCH:(c + 1) * KCH, :] = jnp.exp2(t - m).astype(p_scr.dtype)

    macc = jnp.full((SUBLANE, TQ), -jnp.inf, _F32)
    for c in range(nck):
        macc = jnp.maximum(macc, score_chunk(0, c))
    outs, maps = [], []
    for n in range(nmaps):
        h = n // 2
        m = jnp.max(macc, axis=0, keepdims=True)
        macc = jnp.full((SUBLANE, TQ), -jnp.inf, _F32)
        for c in range(nck):
            if n + 1 < nmaps:
                macc = jnp.maximum(macc, score_chunk(n + 1, c))
            prob_chunk(n, c, m)
        v_t = jnp.concatenate([vt_ref[blk, h * HEAD_DIM:(h + 1) * HEAD_DIM, :]
                               for blk in range(seq // LANE)], axis=1)
        o_t = jnp.dot(_with_ones_rows(v_t), p_scr[n % 2], preferred_element_type=_F32)
        maps.append(o_t[:HEAD_DIM] / o_t[HEAD_DIM:HEAD_DIM + 1])
        if n % 2 == 1:
            o = maps[0] - lam * maps[1]
            o = o * lax.rsqrt(jnp.mean(o * o, axis=0, keepdims=True) + EPS)
            outs.append(o * g_ref[...] * (1.0 - lam_init))
            maps = []
    o_ref[...] = jnp.concatenate(outs, axis=0).T.astype(o_ref.dtype)


def _attn_b(qb, kb, vbt, bias, lam_v, gain, bsz, seq, lam_init):
    nq = seq // TQ
    return pl.pallas_call(
        functools.partial(_attn_b_kernel, seq=seq, lam_init=lam_init),
        grid=(bsz, nq),
        in_specs=[
            pl.BlockSpec((TQ, B_W), lambda b, i: (b * nq + i, 0)),
            pl.BlockSpec((seq, B_W), lambda b, i: (b, 0)),
            pl.BlockSpec((seq // LANE, B_W, LANE), lambda b, i: (b, 0, 0)),
            pl.BlockSpec(bias.shape, lambda b, i: (0, 0, 0, 0)),
            pl.BlockSpec(lam_v.shape, lambda b, i: (0, 0)),
            pl.BlockSpec(gain.shape, lambda b, i: (0, 0)),
        ],
        out_specs=pl.BlockSpec((TQ, B_W), lambda b, i: (b * nq + i, 0)),
        out_shape=jax.ShapeDtypeStruct(qb.shape, qb.dtype),
        scratch_shapes=[pltpu.VMEM((2, seq, TQ), _F32), pltpu.VMEM((2, seq, TQ), _MXU_DTYPE)],
        compiler_params=_params(("parallel", "arbitrary")),
        name="attn_diff",
    )(qb, kb, vbt, bias, lam_v, gain)


def _attn_c_kernel(q_ref, k_ref, vt_ref, bias_ref, o_ref, *, seq):
    i = pl.program_id(1)
    rows = seq // GRID_W
    wrow = jnp.clip(i * C_QROWS - NA_ROWS // 2, 0, rows - C_WROWS)
    ws = pl.multiple_of(wrow * GRID_W, TQ)
    kwin = k_ref[pl.ds(ws, C_KEYS), :]
    wb = ws // LANE
    vwin_t = jnp.concatenate([vt_ref[wb + j] for j in range(C_KEYS // LANE)], axis=1)
    q_t = _transpose_bf16(q_ref[...])
    ts = []
    for n in range(C_HEADS):
        g, h = divmod(n, 2)
        qm = _row_band(q_t[g * LANE:(g + 1) * LANE], h * HEAD_DIM, HEAD_DIM)
        ts.append(jnp.dot(kwin[:, g * LANE:(g + 1) * LANE], qm,
                          preferred_element_type=_F32) + bias_ref[0, n])
    outs = []
    for n in range(C_HEADS):
        m = jnp.max(ts[n], axis=0, keepdims=True)
        p = jnp.exp2(ts[n] - m).astype(_MXU_DTYPE)
        o_t = jnp.dot(_with_ones_rows(vwin_t[n * HEAD_DIM:(n + 1) * HEAD_DIM]), p,
                      preferred_element_type=_F32)
        outs.append(o_t[:HEAD_DIM] / o_t[HEAD_DIM:HEAD_DIM + 1])
    o_ref[...] = jnp.concatenate(outs, axis=0).T.astype(o_ref.dtype)


def _attn_c(qc, kc, vct, bias, bsz, seq):
    nq = seq // TQ
    case = lambda i: (i > 0).astype(jnp.int32) + (i == nq - 1).astype(jnp.int32)
    return pl.pallas_call(
        functools.partial(_attn_c_kernel, seq=seq),
        grid=(bsz, nq),
        in_specs=[
            pl.BlockSpec((TQ, C_W), lambda b, i: (b * nq + i, 0)),
            pl.BlockSpec((seq, C_W), lambda b, i: (b, 0)),
            pl.BlockSpec((seq // LANE, C_W, LANE), lambda b, i: (b, 0, 0)),
            pl.BlockSpec((1, C_HEADS, C_KEYS, TQ), lambda b, i: (case(i), 0, 0, 0)),
        ],
        out_specs=pl.BlockSpec((TQ, C_W), lambda b, i: (b * nq + i, 0)),
        out_shape=jax.ShapeDtypeStruct(qc.shape, qc.dtype),
        compiler_params=_params(("parallel", "arbitrary")),
        name="attn_nbr",
    )(qc, kc, vct, bias)


def _outproj_kernel(oa_ref, ob_ref, oc_ref, x_ref, g_ref, wo_ref, lng_ref, lnb_ref, out_ref):
    o = jnp.concatenate([oa_ref[...], ob_ref[...], oc_ref[...]], axis=-1)
    y = jnp.dot(o, wo_ref[...], preferred_element_type=_F32)
    z = ALPHA * x_ref[...] + (1.0 + g_ref[0]) * y
    out_ref[...] = _ln_rows(z) * lng_ref[...] + lnb_ref[...]


def _outproj(oa, ob, oc, x, gate, wo, lng, lnb, seq):
    t, d = x.shape
    tok = lambda w: pl.BlockSpec((TM, w), lambda i: (i, 0))
    mod = pl.BlockSpec((1, 1, d), lambda i: ((i * TM) // seq, 0, 0))
    vec = pl.BlockSpec((1, d), lambda i: (0, 0))
    return pl.pallas_call(
        _outproj_kernel,
        grid=(t // TM,),
        in_specs=[tok(A_Q_W), tok(B_W), tok(C_W), tok(d), mod,
                  pl.BlockSpec((d, d), lambda i: (0, 0)), vec, vec],
        out_specs=tok(d),
        out_shape=jax.ShapeDtypeStruct((t, d), _F32),
        compiler_params=_params(("parallel",)),
        name="outproj_ln",
    )(oa, ob, oc, x, gate, wo, lng, lnb)


def _ffn_kernel(x_ref, sc_ref, sh_ref, g_ref, w1_ref, w2_ref, lng_ref, lnb_ref, out_ref):
    x = x_ref[...]
    h = (_ln_rows(x) * (1.0 + sc_ref[0]) + sh_ref[0]).astype(_MXU_DTYPE)
    f = jnp.zeros(x.shape, _F32)
    for c in range(w1_ref.shape[1] // FF_CHUNK):
        u = jnp.dot(h, w1_ref[:, c * FF_CHUNK:(c + 1) * FF_CHUNK], preferred_element_type=_F32)
        u = jnp.square(jnp.maximum(u, 0.0)).astype(_MXU_DTYPE)
        f = f + jnp.dot(u, w2_ref[c * FF_CHUNK:(c + 1) * FF_CHUNK, :], preferred_element_type=_F32)
    z = ALPHA * x + (1.0 + g_ref[0]) * f
    out_ref[...] = _ln_rows(z) * lng_ref[...] + lnb_ref[...]


def _ffn(x, sc, sh, gate, w1, w2, lng, lnb, seq):
    t, d = x.shape
    dff = w1.shape[1]
    tok = pl.BlockSpec((TM, d), lambda i: (i, 0))
    mod = pl.BlockSpec((1, 1, d), lambda i: ((i * TM) // seq, 0, 0))
    vec = pl.BlockSpec((1, d), lambda i: (0, 0))
    return pl.pallas_call(
        _ffn_kernel,
        grid=(t // TM,),
        in_specs=[tok, mod, mod, mod,
                  pl.BlockSpec((d, dff), lambda i: (0, 0)),
                  pl.BlockSpec((dff, d), lambda i: (0, 0)), vec, vec],
        out_specs=tok,
        out_shape=jax.ShapeDtypeStruct((t, d), _F32),
        compiler_params=_params(("parallel",)),
        name="ffn_ln",
    )(x, sc, sh, gate, w1, w2, lng, lnb)


def _t5_bucket(rel):
    half = T5_BUCKETS // 2
    exact = half // 2
    n = jnp.abs(rel)
    large = exact + (jnp.log(jnp.maximum(n, 1).astype(jnp.float32) / exact)
                     / math.log(T5_MAX_DIST / exact) * (half - exact)).astype(jnp.int32)
    large = jnp.minimum(large, half - 1)
    return (rel > 0).astype(jnp.int32) * half + jnp.where(n < exact, n, large)


def _t5_lut(tab, rel):
    onehot = _t5_bucket(rel)[None, :] == jnp.arange(T5_BUCKETS)[:, None]
    return jnp.sum(jnp.where(onehot[:, None, :], tab.astype(_F32)[:, :, None], 0.0), axis=0)


def _toeplitz(w, nrows):
    h, p = w.shape
    m = jnp.tile(w, (1, nrows + 1))[:, :nrows * (p + 1)].reshape(h, nrows, p + 1)
    return m[:, :, :TQ][:, :, ::-1]


def _bias_tables_a(tab):
    span = A_KEYS + TQ
    out = []
    for off in (0, -WINDOW, -2 * WINDOW):
        rel = jnp.arange(span) + (off - (TQ - 1))
        lut = jnp.where((jnp.abs(rel) <= WINDOW)[None, :], _t5_lut(tab, rel), NEG)
        out.append(_toeplitz(lut[jnp.asarray(A_PERM)], A_KEYS))
    return jnp.stack(out) * LOG2E


def _bias_tables_b(tab):
    nrows = 5 * KCH
    rel = jnp.arange(nrows + TQ) - (2 * KCH + TQ - 1)
    return _toeplitz(_t5_lut(tab, rel), nrows).reshape(B_HEADS, 5, KCH, TQ) * LOG2E


def _bias_tables_c(rpb, rows):
    kh = min(NA_ROWS, rows)
    kc = np.arange(GRID_W)[:, None]
    qc = np.arange(GRID_W)[None, :]
    cstart = np.clip(qc - NA_COLS // 2, 0, GRID_W - NA_COLS)
    col_valid = (kc >= cstart) & (kc < cstart + NA_COLS)
    dc = np.clip(kc - qc + NA_COLS - 1, 0, 2 * NA_COLS - 2)
    onehot = jnp.asarray(dc[None] == np.arange(2 * NA_COLS - 1)[:, None, None])
    tiles = jnp.sum(jnp.where(onehot[None, None], rpb.astype(_F32)[:, :, :, None, None], 0.0), axis=2)
    tiles = jnp.where(jnp.asarray(col_valid)[None, None], tiles, NEG)
    neg_tile = jnp.full((C_HEADS, GRID_W, GRID_W), NEG, _F32)
    out = []
    for r0, w0 in ((0, 0), (NA_ROWS // 2, 0), (rows - C_QROWS, rows - C_WROWS)):
        key_rows = []
        for wr in range(C_WROWS):
            blocks = []
            for qr in range(C_QROWS):
                r, kr = r0 + qr, w0 + wr
                rs = min(max(r - kh // 2, 0), rows - kh)
                blocks.append(tiles[:, kr - r + NA_ROWS - 1] if rs <= kr < rs + kh else neg_tile)
            key_rows.append(jnp.concatenate(blocks, axis=2))
        out.append(jnp.concatenate(key_rows, axis=1))
    return jnp.stack(out) * LOG2E


def _split_w_in(w):
    offs = np.cumsum([0, A_Q_W, A_KV_W, A_KV_W, B_W, B_W, B_W, C_W, C_W, C_W])
    qa, ka, va, qb, kb, vb, qc, kc, vc = [w[:, offs[i]:offs[i + 1]] for i in range(9)]
    qa = qa.reshape(-1, A_HEADS, HEAD_DIM)[:, jnp.asarray(A_PERM)].reshape(-1, A_Q_W)
    qa, qb, qc = qa * (SCALE_A * LOG2E), qb * (SCALE_B * LOG2E), qc * (SCALE_C * LOG2E)
    main = jnp.concatenate([qa, ka, qb, kb, qc, kc], axis=1)
    vt = jnp.concatenate([va, vb, vc], axis=1).T
    return main.astype(_MXU_DTYPE), vt.astype(_MXU_DTYPE)


def _permute_w_out(w):
    wa = w[:A_Q_W].reshape(A_HEADS, HEAD_DIM, -1)[jnp.asarray(A_PERM)].reshape(A_Q_W, -1)
    return jnp.concatenate([wa, w[A_Q_W:]], axis=0).astype(_MXU_DTYPE)


def kernel(x, c, w_ada, b_ada, w_in, w_out, t5_bias, a_sink, diff_lambda, diff_subln,
           nat_rpb, ln_g, ln_b, w_ff1, w_ff2):
    bsz, seq, d = x.shape
    depth = w_in.shape[0]
    assert d == D_MODEL and seq % TM == 0 and seq >= C_KEYS and seq >= A_KEYS
    rows = seq // GRID_W

    mod = _ada(c, w_ada.astype(_MXU_DTYPE), b_ada)
    bias_a = _bias_tables_a(t5_bias[:, :A_HEADS])
    bias_b = _bias_tables_b(t5_bias[:, A_HEADS:])

    xt = x.reshape(bsz * seq, d)
    for l in range(depth):
        sh1, sc1, g1, sh2, sc2, g2 = [mod[l, :, k * d:(k + 1) * d].reshape(bsz, 1, d) for k in range(6)]
        wm, wvt = _split_w_in(w_in[l])
        qa, ka, qb, kb, qc, kc, vat, vbt, vct = _inproj(xt, sc1, sh1, wm, wvt, seq)

        lam_init = 0.8 - 0.6 * math.exp(-0.3 * l)
        gain = jnp.broadcast_to(diff_subln[l].astype(_F32)[:, None], (HEAD_DIM, TQ))
        oa = _attn_a(qa, ka, vat, bias_a, a_sink[l][jnp.asarray(A_PERM)].astype(_F32) * LOG2E, bsz, seq)
        ob = _attn_b(qb, kb, vbt, bias_b, diff_lambda[l].astype(_F32), gain, bsz, seq, lam_init)
        oc = _attn_c(qc, kc, vct, _bias_tables_c(nat_rpb[l], rows), bsz, seq)

        x1 = _outproj(oa, ob, oc, xt, g1, _permute_w_out(w_out[l]),
                      ln_g[l, 0].reshape(1, d), ln_b[l, 0].reshape(1, d), seq)
        xt = _ffn(x1, sc2, sh2, g2, w_ff1[l].astype(_MXU_DTYPE), w_ff2[l].astype(_MXU_DTYPE),
                  ln_g[l, 1].reshape(1, d), ln_b[l, 1].reshape(1, d), seq)
    return xt.reshape(bsz, seq, d)
```

```python
import functools
import math

import jax
import jax.numpy as jnp
import numpy as np
from jax import lax
from jax.experimental import pallas as pl
from jax.experimental.pallas import tpu as pltpu

D_MODEL = 1024
DEPTH = 4
HEAD_DIM = 64
A_HEADS = 6
A_KV_HEADS = 2
B_HEADS = 4
C_HEADS = 6
DIFF_DIM = HEAD_DIM // 2
WINDOW = 128
GRID_W = 64
NA_ROWS = 8
NA_COLS = 16
T5_BUCKETS = 32
T5_MAX_DIST = 128
D_FF = 4 * D_MODEL
ALPHA = (2 * DEPTH) ** 0.25
EPS = 1e-5
NEG = -1e30
A_Q_W = A_HEADS * HEAD_DIM
A_KV_W = A_KV_HEADS * HEAD_DIM
B_W = B_HEADS * HEAD_DIM
C_W = C_HEADS * HEAD_DIM
MAIN_W = A_Q_W + A_KV_W + 2 * B_W + 2 * C_W
VT_W = A_KV_W + B_W + C_W

LANE = 128
SUBLANE = 8
TQ = 256
KCH = 256
TM = 512
FF_CHUNK = 1024
A_KEYS = TQ + 2 * WINDOW
C_QROWS = TQ // GRID_W
C_WROWS = C_QROWS + NA_ROWS
C_KEYS = C_WROWS * GRID_W

ONES_ROWS = 16

SCALE_A = HEAD_DIM ** -0.5
SCALE_B = DIFF_DIM ** -0.5
SCALE_C = HEAD_DIM ** -0.5
LOG2E = math.log2(math.e)

A_PERM = (0, 3, 1, 4, 2, 5)

_MXU_DTYPE = jnp.bfloat16
_F32 = jnp.float32
_VMEM_LIMIT = 56 * 1024 * 1024


def _params(sem, vmem=_VMEM_LIMIT):
    return pltpu.CompilerParams(dimension_semantics=sem, vmem_limit_bytes=vmem)


def _ln_rows(x):
    mu = jnp.mean(x, axis=-1, keepdims=True)
    xc = x - mu
    var = jnp.mean(xc * xc, axis=-1, keepdims=True)
    return xc * lax.rsqrt(var + EPS)


def _ada_kernel(c_ref, w_ref, b_ref, o_ref):
    c = c_ref[...]
    cs = c * (1.0 / (1.0 + jnp.exp(-c)))
    o_ref[0] = jnp.dot(cs.astype(_MXU_DTYPE), w_ref[0], preferred_element_type=_F32) + b_ref[0]


def _ada(c, w_ada, b_ada):
    depth, d, n = w_ada.shape
    bsz = c.shape[0]
    tn = 1024
    return pl.pallas_call(
        _ada_kernel,
        grid=(depth, n // tn),
        in_specs=[
            pl.BlockSpec((bsz, d), lambda l, j: (0, 0)),
            pl.BlockSpec((1, d, tn), lambda l, j: (l, 0, j)),
            pl.BlockSpec((1, 1, tn), lambda l, j: (l, 0, j)),
        ],
        out_specs=pl.BlockSpec((1, bsz, tn), lambda l, j: (l, 0, j)),
        out_shape=jax.ShapeDtypeStruct((depth, bsz, n), _F32),
        compiler_params=_params(("parallel", "parallel")),
        name="ada_mod",
    )(c, w_ada, b_ada.reshape(depth, 1, n))


def _inproj_kernel(x_ref, sc_ref, sh_ref, wm_ref, wvt_ref,
                   qa_ref, ka_ref, qb_ref, kb_ref, qc_ref, kc_ref,
                   vat_ref, vbt_ref, vct_ref):
    h = (_ln_rows(x_ref[...]) * (1.0 + sc_ref[0]) + sh_ref[0]).astype(_MXU_DTYPE)
    main = jnp.dot(h, wm_ref[...], preferred_element_type=_F32).astype(qa_ref.dtype)
    off = 0
    for ref in (qa_ref, ka_ref, qb_ref, kb_ref, qc_ref, kc_ref):
        w = ref.shape[-1]
        ref[...] = main[:, off:off + w]
        off += w
    vt = lax.dot_general(wvt_ref[...], h, (((1,), (1,)), ((), ())),
                         preferred_element_type=_F32).astype(vat_ref.dtype)
    for j in range(vt.shape[1] // LANE):
        blk = vt[:, j * LANE:(j + 1) * LANE]
        vat_ref[j] = blk[0:A_KV_W]
        vbt_ref[j] = blk[A_KV_W:A_KV_W + B_W]
        vct_ref[j] = blk[A_KV_W + B_W:VT_W]


def _inproj(x, sc, sh, wm, wvt, seq):
    t, d = x.shape
    dt = _MXU_DTYPE
    tok = lambda w: pl.BlockSpec((TM, w), lambda i: (i, 0))
    vtb = lambda w: pl.BlockSpec((TM // LANE, w, LANE), lambda i: (i, 0, 0))
    mod = pl.BlockSpec((1, 1, d), lambda i: ((i * TM) // seq, 0, 0))
    widths = (A_Q_W, A_KV_W, B_W, B_W, C_W, C_W)
    vwidths = (A_KV_W, B_W, C_W)
    return pl.pallas_call(
        _inproj_kernel,
        grid=(t // TM,),
        in_specs=[tok(d), mod, mod,
                  pl.BlockSpec((d, MAIN_W), lambda i: (0, 0)),
                  pl.BlockSpec((VT_W, d), lambda i: (0, 0))],
        out_specs=[tok(w) for w in widths] + [vtb(w) for w in vwidths],
        out_shape=[jax.ShapeDtypeStruct((t, w), dt) for w in widths]
        + [jax.ShapeDtypeStruct((t // LANE, w, LANE), dt) for w in vwidths],
        compiler_params=_params(("parallel",)),
        name="ln_inproj",
    )(x, sc, sh, wm, wvt)


def _transpose_bf16(x):
    return x.astype(_F32).T.astype(x.dtype)


def _row_band(x, lo, width):
    row = lax.broadcasted_iota(jnp.int32, x.shape, 0)
    return jnp.where((row >= lo) & (row < lo + width), x, jnp.zeros_like(x))


def _with_ones_rows(v_t):
    ones = jnp.ones((ONES_ROWS, v_t.shape[1]), v_t.dtype)
    return jnp.concatenate([v_t, ones], axis=0)


def _fold_rows(x, op):
    return op(x.reshape(x.shape[0] // SUBLANE, SUBLANE, x.shape[1]), axis=0)


def _attn_a_kernel(sink_ref, q_ref, k_ref, vt_ref, bias_ref, o_ref, *, seq):
    i = pl.program_id(1)
    ws = pl.multiple_of(jnp.clip(i * TQ - WINDOW, 0, seq - A_KEYS), LANE)
    kwin = k_ref[pl.ds(ws, A_KEYS), :]
    wb = ws // LANE
    vwin_t = jnp.concatenate([vt_ref[wb + j] for j in range(A_KEYS // LANE)], axis=1)
    q_t = _transpose_bf16(q_ref[...])
    ts = []
    for n in range(A_HEADS):
        g, h = divmod(n, A_KV_HEADS)
        qm = _row_band(q_t[g * LANE:(g + 1) * LANE], h * HEAD_DIM, HEAD_DIM)
        ts.append(jnp.dot(kwin, qm, preferred_element_type=_F32) + bias_ref[0, n])
    outs = []
    for n in range(A_HEADS):
        h = n % A_KV_HEADS
        sk = sink_ref[n]
        m = jnp.maximum(jnp.max(ts[n], axis=0, keepdims=True), sk)
        p = jnp.exp2(ts[n] - m).astype(_MXU_DTYPE)
        o_t = jnp.dot(_with_ones_rows(vwin_t[h * HEAD_DIM:(h + 1) * HEAD_DIM]), p,
                      preferred_element_type=_F32)
        den = o_t[HEAD_DIM:HEAD_DIM + 1] + jnp.exp2(sk - m)
        outs.append(o_t[:HEAD_DIM] / den)
    o_ref[...] = jnp.concatenate(outs, axis=0).T.astype(o_ref.dtype)


def _attn_a(qa, ka, vat, bias, sink, bsz, seq):
    nq = seq // TQ
    case = lambda i: (i > 0).astype(jnp.int32) + (i == nq - 1).astype(jnp.int32)
    return pl.pallas_call(
        functools.partial(_attn_a_kernel, seq=seq),
        grid=(bsz, nq),
        in_specs=[
            pl.BlockSpec(memory_space=pltpu.SMEM),
            pl.BlockSpec((TQ, A_Q_W), lambda b, i: (b * nq + i, 0)),
            pl.BlockSpec((seq, A_KV_W), lambda b, i: (b, 0)),
            pl.BlockSpec((seq // LANE, A_KV_W, LANE), lambda b, i: (b, 0, 0)),
            pl.BlockSpec((1, A_HEADS, A_KEYS, TQ), lambda b, i: (case(i), 0, 0, 0)),
        ],
        out_specs=pl.BlockSpec((TQ, A_Q_W), lambda b, i: (b * nq + i, 0)),
        out_shape=jax.ShapeDtypeStruct(qa.shape, qa.dtype),
        compiler_params=_params(("parallel", "arbitrary")),
        name="attn_window",
    )(sink, qa, ka, vat, bias)


def _attn_b_kernel(far_ref, q_ref, k_ref, vt_ref, bias_ref, lam_ref, g_ref, o_ref,
                   s0_scr, s1_scr, p0_scr, p1_scr, *, seq, lam_init):
    i = pl.program_id(1)
    nck = seq // KCH
    nmaps = 2 * B_HEADS
    q_t = _transpose_bf16(q_ref[...])
    qms = [_row_band(q_t, n * DIFF_DIM, DIFF_DIM) for n in range(nmaps)]
    lv = lam_ref[...]
    lam = (jnp.exp(jnp.sum(lv[0:1] * lv[1:2], axis=-1, keepdims=True))
           - jnp.exp(jnp.sum(lv[2:3] * lv[3:4], axis=-1, keepdims=True)) + lam_init)

    band0 = jnp.clip(i - 1, 0, nck - 3)
    s_bufs, p_bufs = (s0_scr, s1_scr), (p0_scr, p1_scr)

    def far_bias(n, c):
        return jnp.where(c < i, far_ref[n // 2, 0], far_ref[n // 2, 1])

    def in_band(c):
        return (c >= band0) & (c < band0 + 3)

    def score_chunk(n, c):
        t = jnp.dot(k_ref[c * KCH:(c + 1) * KCH, :], qms[n], preferred_element_type=_F32)
        s_bufs[n % 2][c * KCH:(c + 1) * KCH, :] = t
        return _fold_rows(t, jnp.max) + jnp.where(in_band(c), -jnp.inf, far_bias(n, c))

    def band_fixup(n):
        acc = jnp.full((SUBLANE, TQ), -jnp.inf, _F32)
        for d in range(3):
            c = band0 + d
            rows = pl.ds(pl.multiple_of(c * KCH, KCH), KCH)
            t = s_bufs[n % 2][rows, :] + bias_ref[n // 2, jnp.clip(c - i, -2, 2) + 2]
            s_bufs[n % 2][rows, :] = t
            acc = jnp.maximum(acc, _fold_rows(t, jnp.max))
        return acc

    def prob_chunk(n, c, m):
        shift = m - jnp.where(in_band(c), 0.0, far_bias(n, c))
        t = s_bufs[n % 2][c * KCH:(c + 1) * KCH, :]
        p_bufs[n % 2][c * KCH:(c + 1) * KCH, :] = jnp.exp2(t - shift).astype(_MXU_DTYPE)

    macc = jnp.full((SUBLANE, TQ), -jnp.inf, _F32)
    for c in range(nck):
        macc = jnp.maximum(macc, score_chunk(0, c))
    macc = jnp.maximum(macc, band_fixup(0))
    outs, maps = [], []
    for n in range(nmaps):
        h = n // 2
        m = jnp.max(macc, axis=0, keepdims=True)
        macc = jnp.full((SUBLANE, TQ), -jnp.inf, _F32)
        for c in range(nck):
            if n + 1 < nmaps:
                macc = jnp.maximum(macc, score_chunk(n + 1, c))
            prob_chunk(n, c, m)
        if n + 1 < nmaps:
            macc = jnp.maximum(macc, band_fixup(n + 1))
        v_t = jnp.concatenate([vt_ref[blk, h * HEAD_DIM:(h + 1) * HEAD_DIM, :]
                               for blk in range(seq // LANE)], axis=1)
        o_t = jnp.dot(_with_ones_rows(v_t), p_bufs[n % 2][...], preferred_element_type=_F32)
        maps.append(o_t[:HEAD_DIM] / o_t[HEAD_DIM:HEAD_DIM + 1])
        if n % 2 == 1:
            o = maps[0] - lam * maps[1]
            o = o * lax.rsqrt(jnp.mean(o * o, axis=0, keepdims=True) + EPS)
            outs.append(o * g_ref[...] * (1.0 - lam_init))
            maps = []
    o_ref[...] = jnp.concatenate(outs, axis=0).T.astype(o_ref.dtype)


def _attn_b(qb, kb, vbt, bias, lam_v, gain, bsz, seq, lam_init):
    nq = seq // TQ
    assert seq // KCH >= 3
    far = jnp.stack([bias[:, 0, 0, 0], bias[:, 4, 0, 0]], axis=1)
    return pl.pallas_call(
        functools.partial(_attn_b_kernel, seq=seq, lam_init=lam_init),
        grid=(bsz, nq),
        in_specs=[
            pl.BlockSpec(memory_space=pltpu.SMEM),
            pl.BlockSpec((TQ, B_W), lambda b, i: (b * nq + i, 0)),
            pl.BlockSpec((seq, B_W), lambda b, i: (b, 0)),
            pl.BlockSpec((seq // LANE, B_W, LANE), lambda b, i: (b, 0, 0)),
            pl.BlockSpec(bias.shape, lambda b, i: (0, 0, 0, 0)),
            pl.BlockSpec(lam_v.shape, lambda b, i: (0, 0)),
            pl.BlockSpec(gain.shape, lambda b, i: (0, 0)),
        ],
        out_specs=pl.BlockSpec((TQ, B_W), lambda b, i: (b * nq + i, 0)),
        out_shape=jax.ShapeDtypeStruct(qb.shape, qb.dtype),
        scratch_shapes=[pltpu.VMEM((seq, TQ), _F32)] * 2 + [pltpu.VMEM((seq, TQ), _MXU_DTYPE)] * 2,
        compiler_params=_params(("parallel", "arbitrary")),
        name="attn_diff",
    )(far, qb, kb, vbt, bias, lam_v, gain)


def _attn_c_kernel(q_ref, k_ref, vt_ref, bias_ref, o_ref, *, seq):
    i = pl.program_id(1)
    rows = seq // GRID_W
    wrow = jnp.clip(i * C_QROWS - NA_ROWS // 2, 0, rows - C_WROWS)
    ws = pl.multiple_of(wrow * GRID_W, TQ)
    kwin = k_ref[pl.ds(ws, C_KEYS), :]
    wb = ws // LANE
    vwin_t = jnp.concatenate([vt_ref[wb + j] for j in range(C_KEYS // LANE)], axis=1)
    q_t = _transpose_bf16(q_ref[...])
    ts = []
    for n in range(C_HEADS):
        g, h = divmod(n, 2)
        qm = _row_band(q_t[g * LANE:(g + 1) * LANE], h * HEAD_DIM, HEAD_DIM)
        ts.append(jnp.dot(kwin[:, g * LANE:(g + 1) * LANE], qm,
                          preferred_element_type=_F32) + bias_ref[0, n])
    outs = []
    for n in range(C_HEADS):
        m = jnp.max(ts[n], axis=0, keepdims=True)
        p = jnp.exp2(ts[n] - m).astype(_MXU_DTYPE)
        o_t = jnp.dot(_with_ones_rows(vwin_t[n * HEAD_DIM:(n + 1) * HEAD_DIM]), p,
                      preferred_element_type=_F32)
        outs.append(o_t[:HEAD_DIM] / o_t[HEAD_DIM:HEAD_DIM + 1])
    o_ref[...] = jnp.concatenate(outs, axis=0).T.astype(o_ref.dtype)


def _attn_c(qc, kc, vct, bias, bsz, seq):
    nq = seq // TQ
    case = lambda i: (i > 0).astype(jnp.int32) + (i == nq - 1).astype(jnp.int32)
    return pl.pallas_call(
        functools.partial(_attn_c_kernel, seq=seq),
        grid=(bsz, nq),
        in_specs=[
            pl.BlockSpec((TQ, C_W), lambda b, i: (b * nq + i, 0)),
            pl.BlockSpec((seq, C_W), lambda b, i: (b, 0)),
            pl.BlockSpec((seq // LANE, C_W, LANE), lambda b, i: (b, 0, 0)),
            pl.BlockSpec((1, C_HEADS, C_KEYS, TQ), lambda b, i: (case(i), 0, 0, 0)),
        ],
        out_specs=pl.BlockSpec((TQ, C_W), lambda b, i: (b * nq + i, 0)),
        out_shape=jax.ShapeDtypeStruct(qc.shape, qc.dtype),
        compiler_params=_params(("parallel", "arbitrary")),
        name="attn_nbr",
    )(qc, kc, vct, bias)


def _outproj_kernel(oa_ref, ob_ref, oc_ref, x_ref, g_ref, wo_ref, lng_ref, lnb_ref, out_ref):
    o = jnp.concatenate([oa_ref[...], ob_ref[...], oc_ref[...]], axis=-1)
    y = jnp.dot(o, wo_ref[...], preferred_element_type=_F32)
    z = ALPHA * x_ref[...] + (1.0 + g_ref[0]) * y
    out_ref[...] = _ln_rows(z) * lng_ref[...] + lnb_ref[...]


def _outproj(oa, ob, oc, x, gate, wo, lng, lnb, seq):
    t, d = x.shape
    tok = lambda w: pl.BlockSpec((TM, w), lambda i: (i, 0))
    mod = pl.BlockSpec((1, 1, d), lambda i: ((i * TM) // seq, 0, 0))
    vec = pl.BlockSpec((1, d), lambda i: (0, 0))
    return pl.pallas_call(
        _outproj_kernel,
        grid=(t // TM,),
        in_specs=[tok(A_Q_W), tok(B_W), tok(C_W), tok(d), mod,
                  pl.BlockSpec((d, d), lambda i: (0, 0)), vec, vec],
        out_specs=tok(d),
        out_shape=jax.ShapeDtypeStruct((t, d), _F32),
        compiler_params=_params(("parallel",)),
        name="outproj_ln",
    )(oa, ob, oc, x, gate, wo, lng, lnb)


def _ffn_kernel(x_ref, sc_ref, sh_ref, g_ref, w1_ref, w2_ref, lng_ref, lnb_ref, out_ref):
    x = x_ref[...]
    h = (_ln_rows(x) * (1.0 + sc_ref[0]) + sh_ref[0]).astype(_MXU_DTYPE)
    f = jnp.zeros(x.shape, _F32)
    for c in range(w1_ref.shape[1] // FF_CHUNK):
        u = jnp.dot(h, w1_ref[:, c * FF_CHUNK:(c + 1) * FF_CHUNK], preferred_element_type=_F32)
        u = jnp.square(jnp.maximum(u, 0.0)).astype(_MXU_DTYPE)
        f = f + jnp.dot(u, w2_ref[c * FF_CHUNK:(c + 1) * FF_CHUNK, :], preferred_element_type=_F32)
    z = ALPHA * x + (1.0 + g_ref[0]) * f
    out_ref[...] = _ln_rows(z) * lng_ref[...] + lnb_ref[...]


def _ffn(x, sc, sh, gate, w1, w2, lng, lnb, seq):
    t, d = x.shape
    dff = w1.shape[1]
    tok = pl.BlockSpec((TM, d), lambda i: (i, 0))
    mod = pl.BlockSpec((1, 1, d), lambda i: ((i * TM) // seq, 0, 0))
    vec = pl.BlockSpec((1, d), lambda i: (0, 0))
    return pl.pallas_call(
        _ffn_kernel,
        grid=(t // TM,),
        in_specs=[tok, mod, mod, mod,
                  pl.BlockSpec((d, dff), lambda i: (0, 0)),
                  pl.BlockSpec((dff, d), lambda i: (0, 0)), vec, vec],
        out_specs=tok,
        out_shape=jax.ShapeDtypeStruct((t, d), _F32),
        compiler_params=_params(("parallel",)),
        name="ffn_ln",
    )(x, sc, sh, gate, w1, w2, lng, lnb)


def _t5_bucket(rel):
    half = T5_BUCKETS // 2
    exact = half // 2
    n = jnp.abs(rel)
    large = exact + (jnp.log(jnp.maximum(n, 1).astype(jnp.float32) / exact)
                     / math.log(T5_MAX_DIST / exact) * (half - exact)).astype(jnp.int32)
    large = jnp.minimum(large, half - 1)
    return (rel > 0).astype(jnp.int32) * half + jnp.where(n < exact, n, large)


def _t5_lut(tab, rel):
    onehot = _t5_bucket(rel)[None, :] == jnp.arange(T5_BUCKETS)[:, None]
    return jnp.sum(jnp.where(onehot[:, None, :], tab.astype(_F32)[:, :, None], 0.0), axis=0)


def _toeplitz(w, nrows, ncols=TQ):
    h, p = w.shape
    m = jnp.tile(w, (1, nrows + 1))[:, :nrows * (p + 1)].reshape(h, nrows, p + 1)
    return m[:, :, :ncols][:, :, ::-1]


def _bias_tables_a(tab):
    span = A_KEYS + TQ
    out = []
    for off in (0, -WINDOW, -2 * WINDOW):
        rel = jnp.arange(span) + (off - (TQ - 1))
        lut = jnp.where((jnp.abs(rel) <= WINDOW)[None, :], _t5_lut(tab, rel), NEG)
        out.append(_toeplitz(lut[jnp.asarray(A_PERM)], A_KEYS))
    return jnp.stack(out) * LOG2E


def _bias_tables_b(tab):
    nrows = 5 * KCH
    rel = jnp.arange(nrows + TQ) - (2 * KCH + TQ - 1)
    return _toeplitz(_t5_lut(tab, rel), nrows).reshape(B_HEADS, 5, KCH, TQ) * LOG2E


def _bias_tables_c(rpb, rows):
    kh = min(NA_ROWS, rows)
    kc = np.arange(GRID_W)[:, None]
    qc = np.arange(GRID_W)[None, :]
    cstart = np.clip(qc - NA_COLS // 2, 0, GRID_W - NA_COLS)
    col_valid = (kc >= cstart) & (kc < cstart + NA_COLS)
    nrel = 2 * NA_COLS - 1
    lpad = GRID_W - NA_COLS
    w = jnp.pad(rpb.astype(_F32).reshape(-1, nrel), ((0, 0), (lpad, 2 * GRID_W - lpad - nrel)))
    tiles = _toeplitz(w, GRID_W, GRID_W).reshape(C_HEADS, 2 * NA_ROWS - 1, GRID_W, GRID_W)
    tiles = jnp.where(jnp.asarray(col_valid)[None, None], tiles, NEG)
    neg_tile = jnp.full((C_HEADS, GRID_W, GRID_W), NEG, _F32)
    out = []
    for r0, w0 in ((0, 0), (NA_ROWS // 2, 0), (rows - C_QROWS, rows - C_WROWS)):
        key_rows = []
        for wr in range(C_WROWS):
            blocks = []
            for qr in range(C_QROWS):
                r, kr = r0 + qr, w0 + wr
                rs = min(max(r - kh // 2, 0), rows - kh)
                blocks.append(tiles[:, kr - r + NA_ROWS - 1] if rs <= kr < rs + kh else neg_tile)
            key_rows.append(jnp.concatenate(blocks, axis=2))
        out.append(jnp.concatenate(key_rows, axis=1))
    return jnp.stack(out) * LOG2E


def _split_w_in(w):
    offs = np.cumsum([0, A_Q_W, A_KV_W, A_KV_W, B_W, B_W, B_W, C_W, C_W, C_W])
    qa, ka, va, qb, kb, vb, qc, kc, vc = [w[:, offs[i]:offs[i + 1]] for i in range(9)]
    qa = qa.reshape(-1, A_HEADS, HEAD_DIM)[:, jnp.asarray(A_PERM)].reshape(-1, A_Q_W)
    qa, qb, qc = qa * (SCALE_A * LOG2E), qb * (SCALE_B * LOG2E), qc * (SCALE_C * LOG2E)
    main = jnp.concatenate([qa, ka, qb, kb, qc, kc], axis=1)
    vt = jnp.concatenate([va, vb, vc], axis=1).T
    return main.astype(_MXU_DTYPE), vt.astype(_MXU_DTYPE)


def _permute_w_out(w):
    wa = w[:A_Q_W].reshape(A_HEADS, HEAD_DIM, -1)[jnp.asarray(A_PERM)].reshape(A_Q_W, -1)
    return jnp.concatenate([wa, w[A_Q_W:]], axis=0).astype(_MXU_DTYPE)


def kernel(x, c, w_ada, b_ada, w_in, w_out, t5_bias, a_sink, diff_lambda, diff_subln,
           nat_rpb, ln_g, ln_b, w_ff1, w_ff2):
    bsz, seq, d = x.shape
    depth = w_in.shape[0]
    assert d == D_MODEL and seq % TM == 0 and seq >= C_KEYS and seq >= A_KEYS
    rows = seq // GRID_W

    mod = _ada(c, w_ada.astype(_MXU_DTYPE), b_ada)
    bias_a = _bias_tables_a(t5_bias[:, :A_HEADS])
    bias_b = _bias_tables_b(t5_bias[:, A_HEADS:])

    xt = x.reshape(bsz * seq, d)
    for l in range(depth):
        sh1, sc1, g1, sh2, sc2, g2 = [mod[l, :, k * d:(k + 1) * d].reshape(bsz, 1, d) for k in range(6)]
        wm, wvt = _split_w_in(w_in[l])
        qa, ka, qb, kb, qc, kc, vat, vbt, vct = _inproj(xt, sc1, sh1, wm, wvt, seq)

        lam_init = 0.8 - 0.6 * math.exp(-0.3 * l)
        gain = jnp.broadcast_to(diff_subln[l].astype(_F32)[:, None], (HEAD_DIM, TQ))
        oa = _attn_a(qa, ka, vat, bias_a, a_sink[l][jnp.asarray(A_PERM)].astype(_F32) * LOG2E, bsz, seq)
        ob = _attn_b(qb, kb, vbt, bias_b, diff_lambda[l].astype(_F32), gain, bsz, seq, lam_init)
        oc = _attn_c(qc, kc, vct, _bias_tables_c(nat_rpb[l], rows), bsz, seq)

        x1 = _outproj(oa, ob, oc, xt, g1, _permute_w_out(w_out[l]),
                      ln_g[l, 0].reshape(1, d), ln_b[l, 0].reshape(1, d), seq)
        xt = _ffn(x1, sc2, sh2, g2, w_ff1[l].astype(_MXU_DTYPE), w_ff2[l].astype(_MXU_DTYPE),
                  ln_g[l, 1].reshape(1, d), ln_b[l, 1].reshape(1, d), seq)
    return xt.reshape(bsz, seq, d)
```

```python
import functools
import math

import jax
import jax.numpy as jnp
import numpy as np
from jax import lax
from jax.experimental import pallas as pl
from jax.experimental.pallas import tpu as pltpu

D_MODEL = 1024
DEPTH = 4
HEAD_DIM = 64
A_HEADS = 6
A_KV_HEADS = 2
B_HEADS = 4
C_HEADS = 6
DIFF_DIM = HEAD_DIM // 2
WINDOW = 128
GRID_W = 64
NA_ROWS = 8
NA_COLS = 16
T5_BUCKETS = 32
T5_MAX_DIST = 128
D_FF = 4 * D_MODEL
ALPHA = (2 * DEPTH) ** 0.25
EPS = 1e-5
NEG = -1e30
A_Q_W = A_HEADS * HEAD_DIM
A_KV_W = A_KV_HEADS * HEAD_DIM
B_W = B_HEADS * HEAD_DIM
C_W = C_HEADS * HEAD_DIM
MAIN_W = A_Q_W + A_KV_W + 2 * B_W + 2 * C_W
VT_W = A_KV_W + B_W + C_W

LANE = 128
SUBLANE = 8
TQ = 256
KCH = 256
TM = 512
FF_CHUNK = 1024
A_KEYS = TQ + 2 * WINDOW
C_QROWS = TQ // GRID_W
C_WROWS = C_QROWS + NA_ROWS
C_KEYS = C_WROWS * GRID_W

ONES_ROWS = 16

SCALE_A = HEAD_DIM ** -0.5
SCALE_B = DIFF_DIM ** -0.5
SCALE_C = HEAD_DIM ** -0.5
LOG2E = math.log2(math.e)

A_PERM = (0, 3, 1, 4, 2, 5)

_MXU_DTYPE = jnp.bfloat16
_F32 = jnp.float32
_VMEM_LIMIT = 56 * 1024 * 1024


def _params(sem, vmem=_VMEM_LIMIT):
    return pltpu.CompilerParams(dimension_semantics=sem, vmem_limit_bytes=vmem)


def _ln_rows(x):
    mu = jnp.mean(x, axis=-1, keepdims=True)
    xc = x - mu
    var = jnp.mean(xc * xc, axis=-1, keepdims=True)
    return xc * lax.rsqrt(var + EPS)


def _ada_kernel(c_ref, w_ref, b_ref, o_ref):
    c = c_ref[...]
    cs = c * (1.0 / (1.0 + jnp.exp(-c)))
    o_ref[0] = jnp.dot(cs.astype(_MXU_DTYPE), w_ref[0], preferred_element_type=_F32) + b_ref[0]


def _ada(c, w_ada, b_ada):
    depth, d, n = w_ada.shape
    bsz = c.shape[0]
    tn = 1024
    return pl.pallas_call(
        _ada_kernel,
        grid=(depth, n // tn),
        in_specs=[
            pl.BlockSpec((bsz, d), lambda l, j: (0, 0)),
            pl.BlockSpec((1, d, tn), lambda l, j: (l, 0, j)),
            pl.BlockSpec((1, 1, tn), lambda l, j: (l, 0, j)),
        ],
        out_specs=pl.BlockSpec((1, bsz, tn), lambda l, j: (l, 0, j)),
        out_shape=jax.ShapeDtypeStruct((depth, bsz, n), _F32),
        compiler_params=_params(("parallel", "parallel")),
        name="ada_mod",
    )(c, w_ada, b_ada.reshape(depth, 1, n))


def _inproj_kernel(x_ref, sc_ref, sh_ref, wm_ref, wvt_ref,
                   qa_ref, ka_ref, qb_ref, kb_ref, qc_ref, kc_ref,
                   vat_ref, vbt_ref, vct_ref):
    h = (_ln_rows(x_ref[...]) * (1.0 + sc_ref[0]) + sh_ref[0]).astype(_MXU_DTYPE)
    main = jnp.dot(h, wm_ref[...], preferred_element_type=_F32).astype(qa_ref.dtype)
    off = 0
    for ref in (qa_ref, ka_ref, qb_ref, kb_ref, qc_ref, kc_ref):
        w = ref.shape[-1]
        ref[...] = main[:, off:off + w]
        off += w
    vt = lax.dot_general(wvt_ref[...], h, (((1,), (1,)), ((), ())),
                         preferred_element_type=_F32).astype(vat_ref.dtype)
    for j in range(vt.shape[1] // LANE):
        blk = vt[:, j * LANE:(j + 1) * LANE]
        vat_ref[j] = blk[0:A_KV_W]
        vbt_ref[j] = blk[A_KV_W:A_KV_W + B_W]
        vct_ref[j] = blk[A_KV_W + B_W:VT_W]


def _inproj(x, sc, sh, wm, wvt, seq):
    t, d = x.shape
    dt = _MXU_DTYPE
    tok = lambda w: pl.BlockSpec((TM, w), lambda i: (i, 0))
    vtb = lambda w: pl.BlockSpec((TM // LANE, w, LANE), lambda i: (i, 0, 0))
    mod = pl.BlockSpec((1, 1, d), lambda i: ((i * TM) // seq, 0, 0))
    widths = (A_Q_W, A_KV_W, B_W, B_W, C_W, C_W)
    vwidths = (A_KV_W, B_W, C_W)
    return pl.pallas_call(
        _inproj_kernel,
        grid=(t // TM,),
        in_specs=[tok(d), mod, mod,
                  pl.BlockSpec((d, MAIN_W), lambda i: (0, 0)),
                  pl.BlockSpec((VT_W, d), lambda i: (0, 0))],
        out_specs=[tok(w) for w in widths] + [vtb(w) for w in vwidths],
        out_shape=[jax.ShapeDtypeStruct((t, w), dt) for w in widths]
        + [jax.ShapeDtypeStruct((t // LANE, w, LANE), dt) for w in vwidths],
        compiler_params=_params(("parallel",)),
        name="ln_inproj",
    )(x, sc, sh, wm, wvt)


def _transpose_bf16(x):
    return x.astype(_F32).T.astype(x.dtype)


def _row_band(x, lo, width):
    row = lax.broadcasted_iota(jnp.int32, x.shape, 0)
    return jnp.where((row >= lo) & (row < lo + width), x, jnp.zeros_like(x))


def _with_ones_rows(v_t):
    ones = jnp.ones((ONES_ROWS, v_t.shape[1]), v_t.dtype)
    return jnp.concatenate([v_t, ones], axis=0)


def _fold_rows(x, op):
    return op(x.reshape(x.shape[0] // SUBLANE, SUBLANE, x.shape[1]), axis=0)


def _attn_a_kernel(sink_ref, q_ref, k_ref, vt_ref, bias_ref, o_ref, *, seq):
    i = pl.program_id(1)
    ws = pl.multiple_of(jnp.clip(i * TQ - WINDOW, 0, seq - A_KEYS), LANE)
    kwin = k_ref[pl.ds(ws, A_KEYS), :]
    wb = ws // LANE
    vwin_t = jnp.concatenate([vt_ref[wb + j] for j in range(A_KEYS // LANE)], axis=1)
    q_t = _transpose_bf16(q_ref[...])
    ts = []
    for n in range(A_HEADS):
        g, h = divmod(n, A_KV_HEADS)
        qm = _row_band(q_t[g * LANE:(g + 1) * LANE], h * HEAD_DIM, HEAD_DIM)
        ts.append(jnp.dot(kwin, qm, preferred_element_type=_F32) + bias_ref[0, n])
    outs = []
    for n in range(A_HEADS):
        h = n % A_KV_HEADS
        sk = sink_ref[n]
        m = jnp.maximum(jnp.max(ts[n], axis=0, keepdims=True), sk)
        p = jnp.exp2(ts[n] - m).astype(_MXU_DTYPE)
        o_t = jnp.dot(_with_ones_rows(vwin_t[h * HEAD_DIM:(h + 1) * HEAD_DIM]), p,
                      preferred_element_type=_F32)
        den = o_t[HEAD_DIM:HEAD_DIM + 1] + jnp.exp2(sk - m)
        outs.append(o_t[:HEAD_DIM] / den)
    o_ref[...] = jnp.concatenate(outs, axis=0).T.astype(o_ref.dtype)


def _attn_a(qa, ka, vat, bias, sink, bsz, seq):
    nq = seq // TQ
    case = lambda i: (i > 0).astype(jnp.int32) + (i == nq - 1).astype(jnp.int32)
    return pl.pallas_call(
        functools.partial(_attn_a_kernel, seq=seq),
        grid=(bsz, nq),
        in_specs=[
            pl.BlockSpec(memory_space=pltpu.SMEM),
            pl.BlockSpec((TQ, A_Q_W), lambda b, i: (b * nq + i, 0)),
            pl.BlockSpec((seq, A_KV_W), lambda b, i: (b, 0)),
            pl.BlockSpec((seq // LANE, A_KV_W, LANE), lambda b, i: (b, 0, 0)),
            pl.BlockSpec((1, A_HEADS, A_KEYS, TQ), lambda b, i: (case(i), 0, 0, 0)),
        ],
        out_specs=pl.BlockSpec((TQ, A_Q_W), lambda b, i: (b * nq + i, 0)),
        out_shape=jax.ShapeDtypeStruct(qa.shape, qa.dtype),
        compiler_params=_params(("parallel", "arbitrary")),
        name="attn_window",
    )(sink, qa, ka, vat, bias)


def _attn_b_kernel(far_ref, q_ref, k_ref, vt_ref, bias_ref, lam_ref, g_ref, o_ref,
                   s0_scr, s1_scr, p0_scr, p1_scr, *, seq, lam_init):
    i = pl.program_id(1)
    nck = seq // KCH
    nmaps = 2 * B_HEADS
    q_t = _transpose_bf16(q_ref[...])
    qms = [_row_band(q_t, n * DIFF_DIM, DIFF_DIM) for n in range(nmaps)]
    lv = lam_ref[...]
    lam = (jnp.exp(jnp.sum(lv[0:1] * lv[1:2], axis=-1, keepdims=True))
           - jnp.exp(jnp.sum(lv[2:3] * lv[3:4], axis=-1, keepdims=True)) + lam_init)

    band0 = jnp.clip(i - 1, 0, nck - 3)
    s_bufs, p_bufs = (s0_scr, s1_scr), (p0_scr, p1_scr)

    def far_bias(n, c):
        return jnp.where(c < i, far_ref[n // 2, 0], far_ref[n // 2, 1])

    def in_band(c):
        return (c >= band0) & (c < band0 + 3)

    def score_chunk(n, c):
        t = jnp.dot(k_ref[c * KCH:(c + 1) * KCH, :], qms[n], preferred_element_type=_F32)
        s_bufs[n % 2][c * KCH:(c + 1) * KCH, :] = t
        return _fold_rows(t, jnp.max) + jnp.where(in_band(c), -jnp.inf, far_bias(n, c))

    def band_fixup(n):
        acc = jnp.full((SUBLANE, TQ), -jnp.inf, _F32)
        for d in range(3):
            c = band0 + d
            rows = pl.ds(pl.multiple_of(c * KCH, KCH), KCH)
            t = s_bufs[n % 2][rows, :] + bias_ref[n // 2, jnp.clip(c - i, -2, 2) + 2]
            s_bufs[n % 2][rows, :] = t
            acc = jnp.maximum(acc, _fold_rows(t, jnp.max))
        return acc

    def prob_chunk(n, c, m):
        shift = m - jnp.where(in_band(c), 0.0, far_bias(n, c))
        t = s_bufs[n % 2][c * KCH:(c + 1) * KCH, :]
        p_bufs[n % 2][c * KCH:(c + 1) * KCH, :] = jnp.exp2(t - shift).astype(_MXU_DTYPE)

    macc = jnp.full((SUBLANE, TQ), -jnp.inf, _F32)
    for c in range(nck):
        macc = jnp.maximum(macc, score_chunk(0, c))
    macc = jnp.maximum(macc, band_fixup(0))
    outs, maps = [], []
    for n in range(nmaps):
        h = n // 2
        m = jnp.max(macc, axis=0, keepdims=True)
        macc = jnp.full((SUBLANE, TQ), -jnp.inf, _F32)
        for c in range(nck):
            if n + 1 < nmaps:
                macc = jnp.maximum(macc, score_chunk(n + 1, c))
            prob_chunk(n, c, m)
        if n + 1 < nmaps:
            macc = jnp.maximum(macc, band_fixup(n + 1))
        v_t = jnp.concatenate([vt_ref[blk, h * HEAD_DIM:(h + 1) * HEAD_DIM, :]
                               for blk in range(seq // LANE)], axis=1)
        o_t = jnp.dot(_with_ones_rows(v_t), p_bufs[n % 2][...], preferred_element_type=_F32)
        maps.append(o_t[:HEAD_DIM] / o_t[HEAD_DIM:HEAD_DIM + 1])
        if n % 2 == 1:
            o = maps[0] - lam * maps[1]
            o = o * lax.rsqrt(jnp.mean(o * o, axis=0, keepdims=True) + EPS)
            outs.append(o * g_ref[...] * (1.0 - lam_init))
            maps = []
    o_ref[...] = jnp.concatenate(outs, axis=0).T.astype(o_ref.dtype)


def _attn_b(qb, kb, vbt, bias, lam_v, gain, bsz, seq, lam_init):
    nq = seq // TQ
    assert seq // KCH >= 3
    far = jnp.stack([bias[:, 0, 0, 0], bias[:, 4, 0, 0]], axis=1)
    return pl.pallas_call(
        functools.partial(_attn_b_kernel, seq=seq, lam_init=lam_init),
        grid=(bsz, nq),
        in_specs=[
            pl.BlockSpec(memory_space=pltpu.SMEM),
            pl.BlockSpec((TQ, B_W), lambda b, i: (b * nq + i, 0)),
            pl.BlockSpec((seq, B_W), lambda b, i: (b, 0)),
            pl.BlockSpec((seq // LANE, B_W, LANE), lambda b, i: (b, 0, 0)),
            pl.BlockSpec(bias.shape, lambda b, i: (0, 0, 0, 0)),
            pl.BlockSpec(lam_v.shape, lambda b, i: (0, 0)),
            pl.BlockSpec(gain.shape, lambda b, i: (0, 0)),
        ],
        out_specs=pl.BlockSpec((TQ, B_W), lambda b, i: (b * nq + i, 0)),
        out_shape=jax.ShapeDtypeStruct(qb.shape, qb.dtype),
        scratch_shapes=[pltpu.VMEM((seq, TQ), _F32)] * 2 + [pltpu.VMEM((seq, TQ), _MXU_DTYPE)] * 2,
        compiler_params=_params(("parallel", "arbitrary")),
        name="attn_diff",
    )(far, qb, kb, vbt, bias, lam_v, gain)


def _attn_c_kernel(q_ref, k_ref, vt_ref, bias_ref, o_ref, *, seq):
    i = pl.program_id(1)
    rows = seq // GRID_W
    wrow = jnp.clip(i * C_QROWS - NA_ROWS // 2, 0, rows - C_WROWS)
    ws = pl.multiple_of(wrow * GRID_W, TQ)
    kwin = k_ref[pl.ds(ws, C_KEYS), :]
    wb = ws // LANE
    vwin_t = jnp.concatenate([vt_ref[wb + j] for j in range(C_KEYS // LANE)], axis=1)
    q_t = _transpose_bf16(q_ref[...])
    ts = []
    for n in range(C_HEADS):
        g, h = divmod(n, 2)
        qm = _row_band(q_t[g * LANE:(g + 1) * LANE], h * HEAD_DIM, HEAD_DIM)
        ts.append(jnp.dot(kwin[:, g * LANE:(g + 1) * LANE], qm,
                          preferred_element_type=_F32) + bias_ref[0, n])
    outs = []
    for n in range(C_HEADS):
        m = jnp.max(ts[n], axis=0, keepdims=True)
        p = jnp.exp2(ts[n] - m).astype(_MXU_DTYPE)
        o_t = jnp.dot(_with_ones_rows(vwin_t[n * HEAD_DIM:(n + 1) * HEAD_DIM]), p,
                      preferred_element_type=_F32)
        outs.append(o_t[:HEAD_DIM] / o_t[HEAD_DIM:HEAD_DIM + 1])
    o_ref[...] = jnp.concatenate(outs, axis=0).T.astype(o_ref.dtype)


def _attn_c(qc, kc, vct, bias, bsz, seq):
    nq = seq // TQ
    case = lambda i: (i > 0).astype(jnp.int32) + (i == nq - 1).astype(jnp.int32)
    return pl.pallas_call(
        functools.partial(_attn_c_kernel, seq=seq),
        grid=(bsz, nq),
        in_specs=[
            pl.BlockSpec((TQ, C_W), lambda b, i: (b * nq + i, 0)),
            pl.BlockSpec((seq, C_W), lambda b, i: (b, 0)),
            pl.BlockSpec((seq // LANE, C_W, LANE), lambda b, i: (b, 0, 0)),
            pl.BlockSpec((1, C_HEADS, C_KEYS, TQ), lambda b, i: (case(i), 0, 0, 0)),
        ],
        out_specs=pl.BlockSpec((TQ, C_W), lambda b, i: (b * nq + i, 0)),
        out_shape=jax.ShapeDtypeStruct(qc.shape, qc.dtype),
        compiler_params=_params(("parallel", "arbitrary")),
        name="attn_nbr",
    )(qc, kc, vct, bias)


def _post_kernel(oa_ref, ob_ref, oc_ref, x_ref, g1_ref, sc_ref, sh_ref, g2_ref,
                 wo_ref, w1_ref, w2_ref, ln_ref, out_ref):
    o = jnp.concatenate([oa_ref[...], ob_ref[...], oc_ref[...]], axis=-1)
    y = jnp.dot(o, wo_ref[...], preferred_element_type=_F32)
    x = _ln_rows(ALPHA * x_ref[...] + (1.0 + g1_ref[0]) * y) * ln_ref[0:1] + ln_ref[1:2]
    h = (_ln_rows(x) * (1.0 + sc_ref[0]) + sh_ref[0]).astype(_MXU_DTYPE)
    f = jnp.zeros(x.shape, _F32)
    for c in range(w1_ref.shape[1] // FF_CHUNK):
        u = jnp.dot(h, w1_ref[:, c * FF_CHUNK:(c + 1) * FF_CHUNK], preferred_element_type=_F32)
        u = jnp.square(jnp.maximum(u, 0.0)).astype(_MXU_DTYPE)
        f = f + jnp.dot(u, w2_ref[c * FF_CHUNK:(c + 1) * FF_CHUNK, :], preferred_element_type=_F32)
    z = ALPHA * x + (1.0 + g2_ref[0]) * f
    out_ref[...] = _ln_rows(z) * ln_ref[2:3] + ln_ref[3:4]


def _post(oa, ob, oc, x, g1, sc, sh, g2, wo, w1, w2, ln, seq):
    t, d = x.shape
    tok = lambda w: pl.BlockSpec((TM, w), lambda i: (i, 0))
    mod = pl.BlockSpec((1, 1, d), lambda i: ((i * TM) // seq, 0, 0))
    resident = lambda a: pl.BlockSpec(a.shape, lambda i: (0, 0), pipeline_mode=pl.Buffered(1))
    return pl.pallas_call(
        _post_kernel,
        grid=(t // TM,),
        in_specs=[tok(A_Q_W), tok(B_W), tok(C_W), tok(d), mod, mod, mod, mod,
                  resident(wo), resident(w1), resident(w2), resident(ln)],
        out_specs=tok(d),
        out_shape=jax.ShapeDtypeStruct((t, d), _F32),
        compiler_params=_params(("parallel",)),
        name="outproj_ffn",
    )(oa, ob, oc, x, g1, sc, sh, g2, wo, w1, w2, ln)


def _t5_bucket(rel):
    half = T5_BUCKETS // 2
    exact = half // 2
    n = jnp.abs(rel)
    large = exact + (jnp.log(jnp.maximum(n, 1).astype(jnp.float32) / exact)
                     / math.log(T5_MAX_DIST / exact) * (half - exact)).astype(jnp.int32)
    large = jnp.minimum(large, half - 1)
    return (rel > 0).astype(jnp.int32) * half + jnp.where(n < exact, n, large)


def _t5_lut(tab, rel):
    onehot = _t5_bucket(rel)[None, :] == jnp.arange(T5_BUCKETS)[:, None]
    return jnp.sum(jnp.where(onehot[:, None, :], tab.astype(_F32)[:, :, None], 0.0), axis=0)


def _toeplitz(w, nrows, ncols=TQ):
    h, p = w.shape
    m = jnp.tile(w, (1, nrows + 1))[:, :nrows * (p + 1)].reshape(h, nrows, p + 1)
    return m[:, :, :ncols][:, :, ::-1]


def _bias_tables_a(tab):
    span = A_KEYS + TQ
    out = []
    for off in (0, -WINDOW, -2 * WINDOW):
        rel = jnp.arange(span) + (off - (TQ - 1))
        lut = jnp.where((jnp.abs(rel) <= WINDOW)[None, :], _t5_lut(tab, rel), NEG)
        out.append(_toeplitz(lut[jnp.asarray(A_PERM)], A_KEYS))
    return jnp.stack(out) * LOG2E


def _bias_tables_b(tab):
    nrows = 3 * KCH
    rel = jnp.arange(nrows + TQ) - (KCH + TQ - 1)
    band = _toeplitz(_t5_lut(tab, rel), nrows).reshape(B_HEADS, 3, KCH, TQ)
    far = _t5_lut(tab, jnp.asarray([-(KCH + 1), KCH + 1]))
    far = jnp.broadcast_to(far[:, :, None, None], (B_HEADS, 2, KCH, TQ))
    return jnp.concatenate([far[:, :1], band, far[:, 1:]], axis=1) * LOG2E


def _bias_tables_c(rpb, rows):
    kh = min(NA_ROWS, rows)
    kc = np.arange(GRID_W)[:, None]
    qc = np.arange(GRID_W)[None, :]
    cstart = np.clip(qc - NA_COLS // 2, 0, GRID_W - NA_COLS)
    col_valid = (kc >= cstart) & (kc < cstart + NA_COLS)
    nrel = 2 * NA_COLS - 1
    lpad = GRID_W - NA_COLS
    w = jnp.pad(rpb.astype(_F32).reshape(-1, nrel), ((0, 0), (lpad, 2 * GRID_W - lpad - nrel)))
    tiles = _toeplitz(w, GRID_W, GRID_W).reshape(C_HEADS, 2 * NA_ROWS - 1, GRID_W, GRID_W)
    tiles = jnp.where(jnp.asarray(col_valid)[None, None], tiles, NEG)
    neg_tile = jnp.full((C_HEADS, GRID_W, GRID_W), NEG, _F32)
    out = []
    for r0, w0 in ((0, 0), (NA_ROWS // 2, 0), (rows - C_QROWS, rows - C_WROWS)):
        key_rows = []
        for wr in range(C_WROWS):
            blocks = []
            for qr in range(C_QROWS):
                r, kr = r0 + qr, w0 + wr
                rs = min(max(r - kh // 2, 0), rows - kh)
                blocks.append(tiles[:, kr - r + NA_ROWS - 1] if rs <= kr < rs + kh else neg_tile)
            key_rows.append(jnp.concatenate(blocks, axis=2))
        out.append(jnp.concatenate(key_rows, axis=1))
    return jnp.stack(out) * LOG2E


def _split_w_in(w):
    offs = np.cumsum([0, A_Q_W, A_KV_W, A_KV_W, B_W, B_W, B_W, C_W, C_W, C_W])
    qa, ka, va, qb, kb, vb, qc, kc, vc = [w[:, offs[i]:offs[i + 1]] for i in range(9)]
    qa = qa.reshape(-1, A_HEADS, HEAD_DIM)[:, jnp.asarray(A_PERM)].reshape(-1, A_Q_W)
    qa, qb, qc = qa * (SCALE_A * LOG2E), qb * (SCALE_B * LOG2E), qc * (SCALE_C * LOG2E)
    main = jnp.concatenate([qa, ka, qb, kb, qc, kc], axis=1)
    vt = jnp.concatenate([va, vb, vc], axis=1).T
    return main.astype(_MXU_DTYPE), vt.astype(_MXU_DTYPE)


def _permute_w_out(w):
    wa = w[:A_Q_W].reshape(A_HEADS, HEAD_DIM, -1)[jnp.asarray(A_PERM)].reshape(A_Q_W, -1)
    return jnp.concatenate([wa, w[A_Q_W:]], axis=0).astype(_MXU_DTYPE)


def kernel(x, c, w_ada, b_ada, w_in, w_out, t5_bias, a_sink, diff_lambda, diff_subln,
           nat_rpb, ln_g, ln_b, w_ff1, w_ff2):
    bsz, seq, d = x.shape
    depth = w_in.shape[0]
    assert d == D_MODEL and seq % TM == 0 and seq >= C_KEYS and seq >= A_KEYS
    rows = seq // GRID_W

    mod = _ada(c, w_ada.astype(_MXU_DTYPE), b_ada)
    bias_a = _bias_tables_a(t5_bias[:, :A_HEADS])
    bias_b = _bias_tables_b(t5_bias[:, A_HEADS:])

    xt = x.reshape(bsz * seq, d)
    for l in range(depth):
        sh1, sc1, g1, sh2, sc2, g2 = [mod[l, :, k * d:(k + 1) * d].reshape(bsz, 1, d) for k in range(6)]
        wm, wvt = _split_w_in(w_in[l])
        qa, ka, qb, kb, qc, kc, vat, vbt, vct = _inproj(xt, sc1, sh1, wm, wvt, seq)

        lam_init = 0.8 - 0.6 * math.exp(-0.3 * l)
        gain = jnp.broadcast_to(diff_subln[l].astype(_F32)[:, None], (HEAD_DIM, TQ))
        oa = _attn_a(qa, ka, vat, bias_a, a_sink[l][jnp.asarray(A_PERM)].astype(_F32) * LOG2E, bsz, seq)
        ob = _attn_b(qb, kb, vbt, bias_b, diff_lambda[l].astype(_F32), gain, bsz, seq, lam_init)
        oc = _attn_c(qc, kc, vct, _bias_tables_c(nat_rpb[l], rows), bsz, seq)

        ln = jnp.stack([ln_g[l, 0], ln_b[l, 0], ln_g[l, 1], ln_b[l, 1]]).astype(_F32)
        xt = _post(oa, ob, oc, xt, g1, sc2, sh2, g2, _permute_w_out(w_out[l]),
                   w_ff1[l].astype(_MXU_DTYPE), w_ff2[l].astype(_MXU_DTYPE), ln, seq)
    return xt.reshape(bsz, seq, d)
```

```python
import functools
import math

import jax
import jax.numpy as jnp
import numpy as np
from jax import lax
from jax.experimental import pallas as pl
from jax.experimental.pallas import tpu as pltpu

D_MODEL = 1024
DEPTH = 4
HEAD_DIM = 64
A_HEADS = 6
A_KV_HEADS = 2
B_HEADS = 4
C_HEADS = 6
DIFF_DIM = HEAD_DIM // 2
WINDOW = 128
GRID_W = 64
NA_ROWS = 8
NA_COLS = 16
T5_BUCKETS = 32
T5_MAX_DIST = 128
D_FF = 4 * D_MODEL
ALPHA = (2 * DEPTH) ** 0.25
EPS = 1e-5
NEG = -1e30
A_Q_W = A_HEADS * HEAD_DIM
A_KV_W = A_KV_HEADS * HEAD_DIM
B_W = B_HEADS * HEAD_DIM
C_W = C_HEADS * HEAD_DIM
MAIN_W = A_Q_W + A_KV_W + 2 * B_W + 2 * C_W
VT_W = A_KV_W + B_W + C_W

LANE = 128
SUBLANE = 8
TQ = 256
KCH = 256
TM = 512
FF_CHUNK = 1024
A_KEYS = TQ + 2 * WINDOW
C_QROWS = TQ // GRID_W
C_WROWS = C_QROWS + NA_ROWS
C_KEYS = C_WROWS * GRID_W

ONES_ROWS = 16

SCALE_A = HEAD_DIM ** -0.5
SCALE_B = DIFF_DIM ** -0.5
SCALE_C = HEAD_DIM ** -0.5
LOG2E = math.log2(math.e)

A_PERM = (0, 3, 1, 4, 2, 5)

_MXU_DTYPE = jnp.bfloat16
_F32 = jnp.float32
_VMEM_LIMIT = 56 * 1024 * 1024


def _params(sem, vmem=_VMEM_LIMIT):
    return pltpu.CompilerParams(dimension_semantics=sem, vmem_limit_bytes=vmem)


def _ln_rows(x):
    mu = jnp.mean(x, axis=-1, keepdims=True)
    xc = x - mu
    var = jnp.mean(xc * xc, axis=-1, keepdims=True)
    return xc * lax.rsqrt(var + EPS)


def _ada_kernel(c_ref, w_ref, b_ref, o_ref):
    c = c_ref[...]
    cs = c * (1.0 / (1.0 + jnp.exp(-c)))
    o_ref[0] = jnp.dot(cs.astype(_MXU_DTYPE), w_ref[0], preferred_element_type=_F32) + b_ref[0]


def _ada(c, w_ada, b_ada):
    depth, d, n = w_ada.shape
    bsz = c.shape[0]
    tn = 1024
    return pl.pallas_call(
        _ada_kernel,
        grid=(depth, n // tn),
        in_specs=[
            pl.BlockSpec((bsz, d), lambda l, j: (0, 0)),
            pl.BlockSpec((1, d, tn), lambda l, j: (l, 0, j)),
            pl.BlockSpec((1, 1, tn), lambda l, j: (l, 0, j)),
        ],
        out_specs=pl.BlockSpec((1, bsz, tn), lambda l, j: (l, 0, j)),
        out_shape=jax.ShapeDtypeStruct((depth, bsz, n), _F32),
        compiler_params=_params(("parallel", "parallel")),
        name="ada_mod",
    )(c, w_ada, b_ada.reshape(depth, 1, n))


def _inproj_kernel(x_ref, sc_ref, sh_ref, wm_ref, wvt_ref,
                   qa_ref, ka_ref, qb_ref, kb_ref, qc_ref, kc_ref,
                   vat_ref, vbt_ref, vct_ref):
    h = (_ln_rows(x_ref[...]) * (1.0 + sc_ref[0]) + sh_ref[0]).astype(_MXU_DTYPE)
    main = jnp.dot(h, wm_ref[...], preferred_element_type=_F32).astype(qa_ref.dtype)
    off = 0
    for ref in (qa_ref, ka_ref, qb_ref, kb_ref, qc_ref, kc_ref):
        w = ref.shape[-1]
        ref[...] = main[:, off:off + w]
        off += w
    vt = lax.dot_general(wvt_ref[...], h, (((1,), (1,)), ((), ())),
                         preferred_element_type=_F32).astype(vat_ref.dtype)
    for j in range(vt.shape[1] // LANE):
        blk = vt[:, j * LANE:(j + 1) * LANE]
        vat_ref[j] = blk[0:A_KV_W]
        vbt_ref[j] = blk[A_KV_W:A_KV_W + B_W]
        vct_ref[j] = blk[A_KV_W + B_W:VT_W]


def _inproj(x, sc, sh, wm, wvt, seq):
    t, d = x.shape
    dt = _MXU_DTYPE
    tok = lambda w: pl.BlockSpec((TM, w), lambda i: (i, 0))
    vtb = lambda w: pl.BlockSpec((TM // LANE, w, LANE), lambda i: (i, 0, 0))
    mod = pl.BlockSpec((1, 1, d), lambda i: ((i * TM) // seq, 0, 0))
    widths = (A_Q_W, A_KV_W, B_W, B_W, C_W, C_W)
    vwidths = (A_KV_W, B_W, C_W)
    return pl.pallas_call(
        _inproj_kernel,
        grid=(t // TM,),
        in_specs=[tok(d), mod, mod,
                  pl.BlockSpec((d, MAIN_W), lambda i: (0, 0)),
                  pl.BlockSpec((VT_W, d), lambda i: (0, 0))],
        out_specs=[tok(w) for w in widths] + [vtb(w) for w in vwidths],
        out_shape=[jax.ShapeDtypeStruct((t, w), dt) for w in widths]
        + [jax.ShapeDtypeStruct((t // LANE, w, LANE), dt) for w in vwidths],
        compiler_params=_params(("parallel",)),
        name="ln_inproj",
    )(x, sc, sh, wm, wvt)


def _transpose_bf16(x):
    return x.astype(_F32).T.astype(x.dtype)


def _row_band(x, lo, width):
    row = lax.broadcasted_iota(jnp.int32, x.shape, 0)
    return jnp.where((row >= lo) & (row < lo + width), x, jnp.zeros_like(x))


def _with_ones_rows(v_t):
    ones = jnp.ones((ONES_ROWS, v_t.shape[1]), v_t.dtype)
    return jnp.concatenate([v_t, ones], axis=0)


def _fold_rows(x, op):
    return op(x.reshape(x.shape[0] // SUBLANE, SUBLANE, x.shape[1]), axis=0)


def _attn_a_kernel(sink_ref, q_ref, k_ref, vt_ref, bias_ref, o_ref, *, seq):
    i = pl.program_id(1)
    ws = pl.multiple_of(jnp.clip(i * TQ - WINDOW, 0, seq - A_KEYS), LANE)
    kwin = k_ref[pl.ds(ws, A_KEYS), :]
    wb = ws // LANE
    vwin_t = jnp.concatenate([vt_ref[wb + j] for j in range(A_KEYS // LANE)], axis=1)
    q_t = _transpose_bf16(q_ref[...])
    ts = []
    for n in range(A_HEADS):
        g, h = divmod(n, A_KV_HEADS)
        qm = _row_band(q_t[g * LANE:(g + 1) * LANE], h * HEAD_DIM, HEAD_DIM)
        ts.append(jnp.dot(kwin, qm, preferred_element_type=_F32) + bias_ref[0, n])
    outs = []
    for n in range(A_HEADS):
        h = n % A_KV_HEADS
        sk = sink_ref[n]
        m = jnp.maximum(jnp.max(ts[n], axis=0, keepdims=True), sk)
        p = jnp.exp2(ts[n] - m).astype(_MXU_DTYPE)
        o_t = jnp.dot(_with_ones_rows(vwin_t[h * HEAD_DIM:(h + 1) * HEAD_DIM]), p,
                      preferred_element_type=_F32)
        den = o_t[HEAD_DIM:HEAD_DIM + 1] + jnp.exp2(sk - m)
        outs.append(o_t[:HEAD_DIM] / den)
    o_ref[...] = jnp.concatenate(outs, axis=0).T.astype(o_ref.dtype)


def _attn_a(qa, ka, vat, bias, sink, bsz, seq):
    nq = seq // TQ
    case = lambda i: (i > 0).astype(jnp.int32) + (i == nq - 1).astype(jnp.int32)
    return pl.pallas_call(
        functools.partial(_attn_a_kernel, seq=seq),
        grid=(bsz, nq),
        in_specs=[
            pl.BlockSpec(memory_space=pltpu.SMEM),
            pl.BlockSpec((TQ, A_Q_W), lambda b, i: (b * nq + i, 0)),
            pl.BlockSpec((seq, A_KV_W), lambda b, i: (b, 0)),
            pl.BlockSpec((seq // LANE, A_KV_W, LANE), lambda b, i: (b, 0, 0)),
            pl.BlockSpec((1, A_HEADS, A_KEYS, TQ), lambda b, i: (case(i), 0, 0, 0)),
        ],
        out_specs=pl.BlockSpec((TQ, A_Q_W), lambda b, i: (b * nq + i, 0)),
        out_shape=jax.ShapeDtypeStruct(qa.shape, qa.dtype),
        compiler_params=_params(("parallel", "arbitrary")),
        name="attn_window",
    )(sink, qa, ka, vat, bias)


def _attn_b_kernel(far_ref, q_ref, k_ref, vt_ref, bias_ref, lam_ref, g_ref, o_ref,
                   s0_scr, s1_scr, p0_scr, p1_scr, *, seq, lam_init):
    i = pl.program_id(1)
    nck = seq // KCH
    nmaps = 2 * B_HEADS
    q_t = _transpose_bf16(q_ref[...])
    qms = [_row_band(q_t, n * DIFF_DIM, DIFF_DIM) for n in range(nmaps)]
    lv = lam_ref[...]
    lam = (jnp.exp(jnp.sum(lv[0:1] * lv[1:2], axis=-1, keepdims=True))
           - jnp.exp(jnp.sum(lv[2:3] * lv[3:4], axis=-1, keepdims=True)) + lam_init)

    band0 = jnp.clip(i - 1, 0, nck - 3)
    s_bufs, p_bufs = (s0_scr, s1_scr), (p0_scr, p1_scr)

    def far_bias(n, c):
        return jnp.where(c < i, far_ref[n // 2, 0], far_ref[n // 2, 1])

    def in_band(c):
        return (c >= band0) & (c < band0 + 3)

    def score_chunk(n, c):
        t = jnp.dot(k_ref[c * KCH:(c + 1) * KCH, :], qms[n], preferred_element_type=_F32)
        s_bufs[n % 2][c * KCH:(c + 1) * KCH, :] = t
        return _fold_rows(t, jnp.max) + jnp.where(in_band(c), -jnp.inf, far_bias(n, c))

    def band_fixup(n):
        acc = jnp.full((SUBLANE, TQ), -jnp.inf, _F32)
        for d in range(3):
            c = band0 + d
            rows = pl.ds(pl.multiple_of(c * KCH, KCH), KCH)
            t = s_bufs[n % 2][rows, :] + bias_ref[n // 2, jnp.clip(c - i, -2, 2) + 2]
            s_bufs[n % 2][rows, :] = t
            acc = jnp.maximum(acc, _fold_rows(t, jnp.max))
        return acc

    def prob_chunk(n, c, m):
        shift = m - jnp.where(in_band(c), 0.0, far_bias(n, c))
        t = s_bufs[n % 2][c * KCH:(c + 1) * KCH, :]
        p_bufs[n % 2][c * KCH:(c + 1) * KCH, :] = jnp.exp2(t - shift).astype(_MXU_DTYPE)

    macc = jnp.full((SUBLANE, TQ), -jnp.inf, _F32)
    for c in range(nck):
        macc = jnp.maximum(macc, score_chunk(0, c))
    macc = jnp.maximum(macc, band_fixup(0))
    outs, maps = [], []
    for n in range(nmaps):
        h = n // 2
        m = jnp.max(macc, axis=0, keepdims=True)
        macc = jnp.full((SUBLANE, TQ), -jnp.inf, _F32)
        for c in range(nck):
            if n + 1 < nmaps:
                macc = jnp.maximum(macc, score_chunk(n + 1, c))
            prob_chunk(n, c, m)
        if n + 1 < nmaps:
            macc = jnp.maximum(macc, band_fixup(n + 1))
        v_t = jnp.concatenate([vt_ref[blk, h * HEAD_DIM:(h + 1) * HEAD_DIM, :]
                               for blk in range(seq // LANE)], axis=1)
        o_t = jnp.dot(_with_ones_rows(v_t), p_bufs[n % 2][...], preferred_element_type=_F32)
        maps.append(o_t[:HEAD_DIM] / o_t[HEAD_DIM:HEAD_DIM + 1])
        if n % 2 == 1:
            o = maps[0] - lam * maps[1]
            o = o * lax.rsqrt(jnp.mean(o * o, axis=0, keepdims=True) + EPS)
            outs.append(o * g_ref[...] * (1.0 - lam_init))
            maps = []
    o_ref[...] = jnp.concatenate(outs, axis=0).T.astype(o_ref.dtype)


def _attn_b(qb, kb, vbt, bias, lam_v, gain, bsz, seq, lam_init):
    nq = seq // TQ
    assert seq // KCH >= 3
    far = jnp.stack([bias[:, 0, 0, 0], bias[:, 4, 0, 0]], axis=1)
    return pl.pallas_call(
        functools.partial(_attn_b_kernel, seq=seq, lam_init=lam_init),
        grid=(bsz, nq),
        in_specs=[
            pl.BlockSpec(memory_space=pltpu.SMEM),
            pl.BlockSpec((TQ, B_W), lambda b, i: (b * nq + i, 0)),
            pl.BlockSpec((seq, B_W), lambda b, i: (b, 0)),
            pl.BlockSpec((seq // LANE, B_W, LANE), lambda b, i: (b, 0, 0)),
            pl.BlockSpec(bias.shape, lambda b, i: (0, 0, 0, 0)),
            pl.BlockSpec(lam_v.shape, lambda b, i: (0, 0)),
            pl.BlockSpec(gain.shape, lambda b, i: (0, 0)),
        ],
        out_specs=pl.BlockSpec((TQ, B_W), lambda b, i: (b * nq + i, 0)),
        out_shape=jax.ShapeDtypeStruct(qb.shape, qb.dtype),
        scratch_shapes=[pltpu.VMEM((seq, TQ), _F32)] * 2 + [pltpu.VMEM((seq, TQ), _MXU_DTYPE)] * 2,
        compiler_params=_params(("parallel", "arbitrary")),
        name="attn_diff",
    )(far, qb, kb, vbt, bias, lam_v, gain)


def _attn_c_kernel(q_ref, k_ref, vt_ref, bias_ref, o_ref, *, seq):
    i = pl.program_id(1)
    rows = seq // GRID_W
    wrow = jnp.clip(i * C_QROWS - NA_ROWS // 2, 0, rows - C_WROWS)
    ws = pl.multiple_of(wrow * GRID_W, TQ)
    kwin = k_ref[pl.ds(ws, C_KEYS), :]
    wb = ws // LANE
    vwin_t = jnp.concatenate([vt_ref[wb + j] for j in range(C_KEYS // LANE)], axis=1)
    q_t = _transpose_bf16(q_ref[...])
    ts = []
    for n in range(C_HEADS):
        g, h = divmod(n, 2)
        qm = _row_band(q_t[g * LANE:(g + 1) * LANE], h * HEAD_DIM, HEAD_DIM)
        ts.append(jnp.dot(kwin[:, g * LANE:(g + 1) * LANE], qm,
                          preferred_element_type=_F32) + bias_ref[0, n])
    outs = []
    for n in range(C_HEADS):
        m = jnp.max(ts[n], axis=0, keepdims=True)
        p = jnp.exp2(ts[n] - m).astype(_MXU_DTYPE)
        o_t = jnp.dot(_with_ones_rows(vwin_t[n * HEAD_DIM:(n + 1) * HEAD_DIM]), p,
                      preferred_element_type=_F32)
        outs.append(o_t[:HEAD_DIM] / o_t[HEAD_DIM:HEAD_DIM + 1])
    o_ref[...] = jnp.concatenate(outs, axis=0).T.astype(o_ref.dtype)


def _attn_c(qc, kc, vct, bias, bsz, seq):
    nq = seq // TQ
    case = lambda i: (i > 0).astype(jnp.int32) + (i == nq - 1).astype(jnp.int32)
    return pl.pallas_call(
        functools.partial(_attn_c_kernel, seq=seq),
        grid=(bsz, nq),
        in_specs=[
            pl.BlockSpec((TQ, C_W), lambda b, i: (b * nq + i, 0)),
            pl.BlockSpec((seq, C_W), lambda b, i: (b, 0)),
            pl.BlockSpec((seq // LANE, C_W, LANE), lambda b, i: (b, 0, 0)),
            pl.BlockSpec((1, C_HEADS, C_KEYS, TQ), lambda b, i: (case(i), 0, 0, 0)),
        ],
        out_specs=pl.BlockSpec((TQ, C_W), lambda b, i: (b * nq + i, 0)),
        out_shape=jax.ShapeDtypeStruct(qc.shape, qc.dtype),
        compiler_params=_params(("parallel", "arbitrary")),
        name="attn_nbr",
    )(qc, kc, vct, bias)


def _attn_ac_kernel(sink_ref, qa_ref, ka_ref, vat_ref, ba_ref, qc_ref, kc_ref, vct_ref, bc_ref,
                    oa_ref, oc_ref, *, seq):
    _attn_a_kernel(sink_ref, qa_ref, ka_ref, vat_ref, ba_ref, oa_ref, seq=seq)
    _attn_c_kernel(qc_ref, kc_ref, vct_ref, bc_ref, oc_ref, seq=seq)


def _attn_ac(qa, ka, vat, bias_a, sink, qc, kc, vct, bias_c, bsz, seq):
    nq = seq // TQ
    case = lambda i: (i > 0).astype(jnp.int32) + (i == nq - 1).astype(jnp.int32)
    qblk = lambda w: pl.BlockSpec((TQ, w), lambda b, i: (b * nq + i, 0))
    kblk = lambda w: pl.BlockSpec((seq, w), lambda b, i: (b, 0))
    vblk = lambda w: pl.BlockSpec((seq // LANE, w, LANE), lambda b, i: (b, 0, 0))
    bblk = lambda heads, keys: pl.BlockSpec((1, heads, keys, TQ), lambda b, i: (case(i), 0, 0, 0))
    return pl.pallas_call(
        functools.partial(_attn_ac_kernel, seq=seq),
        grid=(bsz, nq),
        in_specs=[pl.BlockSpec(memory_space=pltpu.SMEM),
                  qblk(A_Q_W), kblk(A_KV_W), vblk(A_KV_W), bblk(A_HEADS, A_KEYS),
                  qblk(C_W), kblk(C_W), vblk(C_W), bblk(C_HEADS, C_KEYS)],
        out_specs=[qblk(A_Q_W), qblk(C_W)],
        out_shape=[jax.ShapeDtypeStruct(qa.shape, qa.dtype), jax.ShapeDtypeStruct(qc.shape, qc.dtype)],
        compiler_params=_params(("parallel", "arbitrary")),
        name="attn_window_nbr",
    )(sink, qa, ka, vat, bias_a, qc, kc, vct, bias_c)


def _post_kernel(oa_ref, ob_ref, oc_ref, x_ref, g1_ref, sc_ref, sh_ref, g2_ref,
                 wo_ref, w1_ref, w2_ref, ln_ref, out_ref):
    o = jnp.concatenate([oa_ref[...], ob_ref[...], oc_ref[...]], axis=-1)
    y = jnp.dot(o, wo_ref[...], preferred_element_type=_F32)
    x = _ln_rows(ALPHA * x_ref[...] + (1.0 + g1_ref[0]) * y) * ln_ref[0:1] + ln_ref[1:2]
    h = (_ln_rows(x) * (1.0 + sc_ref[0]) + sh_ref[0]).astype(_MXU_DTYPE)
    f = jnp.zeros(x.shape, _F32)
    for c in range(w1_ref.shape[1] // FF_CHUNK):
        u = jnp.dot(h, w1_ref[:, c * FF_CHUNK:(c + 1) * FF_CHUNK], preferred_element_type=_F32)
        u = jnp.square(jnp.maximum(u, 0.0)).astype(_MXU_DTYPE)
        f = f + jnp.dot(u, w2_ref[c * FF_CHUNK:(c + 1) * FF_CHUNK, :], preferred_element_type=_F32)
    z = ALPHA * x + (1.0 + g2_ref[0]) * f
    out_ref[...] = _ln_rows(z) * ln_ref[2:3] + ln_ref[3:4]


def _post(oa, ob, oc, x, g1, sc, sh, g2, wo, w1, w2, ln, seq):
    t, d = x.shape
    tok = lambda w: pl.BlockSpec((TM, w), lambda i: (i, 0))
    mod = pl.BlockSpec((1, 1, d), lambda i: ((i * TM) // seq, 0, 0))
    resident = lambda a: pl.BlockSpec(a.shape, lambda i: (0, 0), pipeline_mode=pl.Buffered(1))
    return pl.pallas_call(
        _post_kernel,
        grid=(t // TM,),
        in_specs=[tok(A_Q_W), tok(B_W), tok(C_W), tok(d), mod, mod, mod, mod,
                  resident(wo), resident(w1), resident(w2), resident(ln)],
        out_specs=tok(d),
        out_shape=jax.ShapeDtypeStruct((t, d), _F32),
        compiler_params=_params(("parallel",)),
        name="outproj_ffn",
    )(oa, ob, oc, x, g1, sc, sh, g2, wo, w1, w2, ln)


def _t5_bucket(rel):
    half = T5_BUCKETS // 2
    exact = half // 2
    n = jnp.abs(rel)
    large = exact + (jnp.log(jnp.maximum(n, 1).astype(jnp.float32) / exact)
                     / math.log(T5_MAX_DIST / exact) * (half - exact)).astype(jnp.int32)
    large = jnp.minimum(large, half - 1)
    return (rel > 0).astype(jnp.int32) * half + jnp.where(n < exact, n, large)


def _t5_lut(tab, rel):
    onehot = _t5_bucket(rel)[None, :] == jnp.arange(T5_BUCKETS)[:, None]
    return jnp.sum(jnp.where(onehot[:, None, :], tab.astype(_F32)[:, :, None], 0.0), axis=0)


def _toeplitz(w, nrows, ncols=TQ):
    h, p = w.shape
    m = jnp.tile(w, (1, nrows + 1))[:, :nrows * (p + 1)].reshape(h, nrows, p + 1)
    return m[:, :, :ncols][:, :, ::-1]


def _bias_tables_a(tab):
    span = A_KEYS + TQ
    out = []
    for off in (0, -WINDOW, -2 * WINDOW):
        rel = jnp.arange(span) + (off - (TQ - 1))
        lut = jnp.where((jnp.abs(rel) <= WINDOW)[None, :], _t5_lut(tab, rel), NEG)
        out.append(_toeplitz(lut[jnp.asarray(A_PERM)], A_KEYS))
    return jnp.stack(out) * LOG2E


def _bias_tables_b(tab):
    nrows = 3 * KCH
    rel = jnp.arange(nrows + TQ) - (KCH + TQ - 1)
    band = _toeplitz(_t5_lut(tab, rel), nrows).reshape(B_HEADS, 3, KCH, TQ)
    far = _t5_lut(tab, jnp.asarray([-(KCH + 1), KCH + 1]))
    far = jnp.broadcast_to(far[:, :, None, None], (B_HEADS, 2, KCH, TQ))
    return jnp.concatenate([far[:, :1], band, far[:, 1:]], axis=1) * LOG2E


def _bias_tables_c(rpb, rows):
    kh = min(NA_ROWS, rows)
    kc = np.arange(GRID_W)[:, None]
    qc = np.arange(GRID_W)[None, :]
    cstart = np.clip(qc - NA_COLS // 2, 0, GRID_W - NA_COLS)
    col_valid = (kc >= cstart) & (kc < cstart + NA_COLS)
    nrel = 2 * NA_COLS - 1
    lpad = GRID_W - NA_COLS
    w = jnp.pad(rpb.astype(_F32).reshape(-1, nrel), ((0, 0), (lpad, 2 * GRID_W - lpad - nrel)))
    tiles = _toeplitz(w, GRID_W, GRID_W).reshape(C_HEADS, 2 * NA_ROWS - 1, GRID_W, GRID_W)
    tiles = jnp.where(jnp.asarray(col_valid)[None, None], tiles, NEG)
    neg_tile = jnp.full((C_HEADS, GRID_W, GRID_W), NEG, _F32)
    out = []
    for r0, w0 in ((0, 0), (NA_ROWS // 2, 0), (rows - C_QROWS, rows - C_WROWS)):
        key_rows = []
        for wr in range(C_WROWS):
            blocks = []
            for qr in range(C_QROWS):
                r, kr = r0 + qr, w0 + wr
                rs = min(max(r - kh // 2, 0), rows - kh)
                blocks.append(tiles[:, kr - r + NA_ROWS - 1] if rs <= kr < rs + kh else neg_tile)
            key_rows.append(jnp.concatenate(blocks, axis=2))
        out.append(jnp.concatenate(key_rows, axis=1))
    return jnp.stack(out) * LOG2E


def _split_w_in(w):
    offs = np.cumsum([0, A_Q_W, A_KV_W, A_KV_W, B_W, B_W, B_W, C_W, C_W, C_W])
    qa, ka, va, qb, kb, vb, qc, kc, vc = [w[:, offs[i]:offs[i + 1]] for i in range(9)]
    qa = qa.reshape(-1, A_HEADS, HEAD_DIM)[:, jnp.asarray(A_PERM)].reshape(-1, A_Q_W)
    qa, qb, qc = qa * (SCALE_A * LOG2E), qb * (SCALE_B * LOG2E), qc * (SCALE_C * LOG2E)
    main = jnp.concatenate([qa, ka, qb, kb, qc, kc], axis=1)
    vt = jnp.concatenate([va, vb, vc], axis=1).T
    return main.astype(_MXU_DTYPE), vt.astype(_MXU_DTYPE)


def _permute_w_out(w):
    wa = w[:A_Q_W].reshape(A_HEADS, HEAD_DIM, -1)[jnp.asarray(A_PERM)].reshape(A_Q_W, -1)
    return jnp.concatenate([wa, w[A_Q_W:]], axis=0).astype(_MXU_DTYPE)


def kernel(x, c, w_ada, b_ada, w_in, w_out, t5_bias, a_sink, diff_lambda, diff_subln,
           nat_rpb, ln_g, ln_b, w_ff1, w_ff2):
    bsz, seq, d = x.shape
    depth = w_in.shape[0]
    assert d == D_MODEL and seq % TM == 0 and seq >= C_KEYS and seq >= A_KEYS
    rows = seq // GRID_W

    mod = _ada(c, w_ada.astype(_MXU_DTYPE), b_ada)
    bias_a = _bias_tables_a(t5_bias[:, :A_HEADS])
    bias_b = _bias_tables_b(t5_bias[:, A_HEADS:])

    xt = x.reshape(bsz * seq, d)
    for l in range(depth):
        sh1, sc1, g1, sh2, sc2, g2 = [mod[l, :, k * d:(k + 1) * d].reshape(bsz, 1, d) for k in range(6)]
        wm, wvt = _split_w_in(w_in[l])
        qa, ka, qb, kb, qc, kc, vat, vbt, vct = _inproj(xt, sc1, sh1, wm, wvt, seq)

        lam_init = 0.8 - 0.6 * math.exp(-0.3 * l)
        gain = jnp.broadcast_to(diff_subln[l].astype(_F32)[:, None], (HEAD_DIM, TQ))
        oa, oc = _attn_ac(qa, ka, vat, bias_a, a_sink[l][jnp.asarray(A_PERM)].astype(_F32) * LOG2E,
                          qc, kc, vct, _bias_tables_c(nat_rpb[l], rows), bsz, seq)
        ob = _attn_b(qb, kb, vbt, bias_b, diff_lambda[l].astype(_F32), gain, bsz, seq, lam_init)

        ln = jnp.stack([ln_g[l, 0], ln_b[l, 0], ln_g[l, 1], ln_b[l, 1]]).astype(_F32)
        xt = _post(oa, ob, oc, xt, g1, sc2, sh2, g2, _permute_w_out(w_out[l]),
                   w_ff1[l].astype(_MXU_DTYPE), w_ff2[l].astype(_MXU_DTYPE), ln, seq)
    return xt.reshape(bsz, seq, d)
```
